```python
import math
import jax, jax.numpy as jnp
from jax import lax
import numpy as np

D_MODEL = 1024
BATCH = 32
SEQ = 2048
DEPTH = 4

SSM_WIDTH = D_MODEL // 2
SSM_GROUP = 16
SSM_GROUPS = SSM_WIDTH // SSM_GROUP
SSM_STATE = 64
HEAD_DIM = 64
ATTN_WIDTH = D_MODEL - SSM_WIDTH
N_Q_HEADS = ATTN_WIDTH // HEAD_DIM
N_KV_HEADS = 2
Q_PER_KV = N_Q_HEADS // N_KV_HEADS
KV_WIDTH = N_KV_HEADS * HEAD_DIM
IN_WIDTH = SSM_WIDTH + ATTN_WIDTH + 2 * KV_WIDTH
WINDOW = 128
BLOCK = 128
N_BUCKETS = 32
MAX_DISTANCE = 128
D_FF = 2816
CONV_WIDTH = 3
EPS = 1e-6
DT_MIN = 1e-3
DT_MAX = 1e-1
NEG_INF = -1e30

kernel_name = 'hybrid_s5_swa_convffn_encoder'


def rms_norm(x, g):
    xf = x.astype(jnp.float32)
    y = xf * lax.rsqrt(jnp.mean(xf * xf, axis=-1, keepdims=True) + EPS)
    return (y * g.astype(jnp.float32)).astype(x.dtype)


def t5_bucket(rel):
    half = N_BUCKETS // 2
    max_exact = half // 2
    ret = jnp.where(rel > 0, half, 0)
    n = jnp.abs(rel)
    nf = jnp.maximum(n, 1).astype(jnp.float32)
    large = max_exact + (jnp.log(nf / max_exact) / math.log(MAX_DISTANCE / max_exact)
                         * (half - max_exact)).astype(jnp.int32)
    large = jnp.minimum(large, half - 1)
    return ret + jnp.where(n < max_exact, n, large)


def s5_scan(u, lam_re, lam_im, log_step, b_re, b_im, c_re, c_im, reverse):
    f32 = jnp.float32
    lam_re, lam_im = lam_re.astype(f32), lam_im.astype(f32)
    dt = jnp.exp(log_step.astype(f32))[:, None]
    mag = jnp.exp(lam_re * dt)
    ang = lam_im * dt
    lb_re, lb_im = mag * jnp.cos(ang), mag * jnp.sin(ang)
    den = lam_re * lam_re + lam_im * lam_im
    nr, ni = lb_re - 1.0, lb_im
    coef_re = (nr * lam_re + ni * lam_im) / den
    coef_im = (ni * lam_re - nr * lam_im) / den
    b_re, b_im = b_re.astype(f32), b_im.astype(f32)
    bb_re = coef_re[..., None] * b_re - coef_im[..., None] * b_im
    bb_im = coef_re[..., None] * b_im + coef_im[..., None] * b_re
    bu_re = jnp.einsum('blgp,gnp->blgn', u, bb_re)
    bu_im = jnp.einsum('blgp,gnp->blgn', u, bb_im)
    a_re = jnp.broadcast_to(lb_re, bu_re.shape)
    a_im = jnp.broadcast_to(lb_im, bu_im.shape)

    def combine(e1, e2):
        a1r, a1i, b1r, b1i = e1
        a2r, a2i, b2r, b2i = e2
        return (a1r * a2r - a1i * a2i,
                a1r * a2i + a1i * a2r,
                a2r * b1r - a2i * b1i + b2r,
                a2r * b1i + a2i * b1r + b2i)

    _, _, s_re, s_im = lax.associative_scan(combine, (a_re, a_im, bu_re, bu_im),
                                            reverse=reverse, axis=1)
    return (jnp.einsum('blgn,gpn->blgp', s_re, c_re.astype(f32))
            - jnp.einsum('blgn,gpn->blgp', s_im, c_im.astype(f32)))


def s5_mixer(u, lam_re, lam_im, log_step, b_re, b_im, c_re, c_im, d, w_glu, b_glu):
    bsz, seq, _ = u.shape
    uf = u.astype(jnp.float32).reshape(bsz, seq, SSM_GROUPS, SSM_GROUP)
    y = (s5_scan(uf, lam_re[0], lam_im[0], log_step[0], b_re[0], b_im[0], c_re[0], c_im[0], False)
         + s5_scan(uf, lam_re[1], lam_im[1], log_step[1], b_re[1], b_im[1], c_re[1], c_im[1], True)
         + d.astype(jnp.float32).reshape(SSM_GROUPS, SSM_GROUP) * uf)
    z = jax.nn.gelu(y.reshape(bsz, seq, SSM_WIDTH), approximate=True)
    out = z * jax.nn.sigmoid(z @ w_glu.astype(jnp.float32) + b_glu.astype(jnp.float32))
    return out.astype(u.dtype)


def banded_attention(q, k, v, sink, rel_bias):
    bsz, seq, _ = q.shape
    nblk = seq // BLOCK
    q = q.reshape(bsz, nblk, BLOCK, N_KV_HEADS, Q_PER_KV, HEAD_DIM)

    def band(t):
        t = t.reshape(bsz, seq, N_KV_HEADS, HEAD_DIM)
        tp = jnp.pad(t, ((0, 0), (BLOCK, BLOCK), (0, 0), (0, 0)))
        tp = tp.reshape(bsz, nblk + 2, BLOCK, N_KV_HEADS, HEAD_DIM)
        return jnp.concatenate([tp[:, :-2], tp[:, 1:-1], tp[:, 2:]], axis=2)

    kb, vb = band(k), band(v)
    s = jnp.einsum('bnqkgd,bnskd->bnkgqs', q, kb).astype(jnp.float32) * (HEAD_DIM ** -0.5)
    qi = jnp.arange(BLOCK, dtype=jnp.int32)[:, None]
    sj = jnp.arange(3 * BLOCK, dtype=jnp.int32)[None, :]
    rel = sj - BLOCK - qi
    bias = rel_bias.astype(jnp.float32)[t5_bucket(rel)]
    bias = bias.transpose(2, 0, 1).reshape(N_KV_HEADS, Q_PER_KV, BLOCK, 3 * BLOCK)
    key_pos = jnp.arange(nblk, dtype=jnp.int32)[:, None] * BLOCK - BLOCK + sj
    valid = (jnp.abs(rel) <= WINDOW)[None] & ((key_pos >= 0) & (key_pos < seq))[:, None, :]
    s = jnp.where(valid[None, :, None, None], s + bias, NEG_INF)
    sink_l = sink.astype(jnp.float32).reshape(1, 1, N_KV_HEADS, Q_PER_KV, 1, 1)
    m = jnp.maximum(jnp.max(s, axis=-1, keepdims=True), sink_l)
    e = jnp.exp(s - m)
    p = e / (jnp.sum(e, axis=-1, keepdims=True) + jnp.exp(sink_l - m))
    o = jnp.einsum('bnkgqs,bnskd->bnqkgd', p.astype(vb.dtype), vb)
    return o.reshape(bsz, seq, ATTN_WIDTH)


def conv_ffn(h, w_up, conv_w, conv_b, w_down):
    a = h @ w_up
    ap = jnp.pad(a, ((0, 0), (1, 1), (0, 0)))
    a = ap[:, :-2] * conv_w[0] + ap[:, 1:-1] * conv_w[1] + ap[:, 2:] * conv_w[2] + conv_b
    value, gate = a[..., :D_FF], a[..., D_FF:]
    return (jax.nn.gelu(gate, approximate=True) * value) @ w_down


def setup_inputs(seed: int = 0) -> dict:
    key = jax.random.key(seed)
    ks = jax.random.split(key, 28)
    nrm = jax.random.normal
    G, N, P = SSM_GROUPS, SSM_STATE, SSM_GROUP
    n_idx = jnp.arange(N, dtype=jnp.float32)

    def gain(k, shape):
        return 1.0 + 0.05 * nrm(k, shape, jnp.float32)

    return {
        'x': nrm(ks[0], (BATCH, SEQ, D_MODEL), jnp.float32),
        'rel_bias': 0.5 * nrm(ks[1], (N_BUCKETS, N_Q_HEADS), jnp.float32),
        'pre_mix_norm': gain(ks[2], (DEPTH, D_MODEL)),
        'w_in': nrm(ks[3], (DEPTH, D_MODEL, IN_WIDTH), jnp.float32) * D_MODEL ** -0.5,
        'lam_re': -0.5 + 0.01 * nrm(ks[4], (DEPTH, 2, G, N), jnp.float32),
        'lam_im': math.pi * n_idx + 0.01 * nrm(ks[5], (DEPTH, 2, G, N), jnp.float32),
        'log_step': jax.random.uniform(ks[6], (DEPTH, 2, G), jnp.float32,
                                       minval=math.log(DT_MIN), maxval=math.log(DT_MAX)),
        'b_re': nrm(ks[7], (DEPTH, 2, G, N, P), jnp.float32) * (2 * P) ** -0.5,
        'b_im': nrm(ks[8], (DEPTH, 2, G, N, P), jnp.float32) * (2 * P) ** -0.5,
        'c_re': nrm(ks[9], (DEPTH, 2, G, P, N), jnp.float32) * (2 * N) ** -0.5,
        'c_im': nrm(ks[10], (DEPTH, 2, G, P, N), jnp.float32) * (2 * N) ** -0.5,
        'ssm_d': nrm(ks[11], (DEPTH, SSM_WIDTH), jnp.float32),
        'w_glu': nrm(ks[12], (DEPTH, SSM_WIDTH, SSM_WIDTH), jnp.float32) * SSM_WIDTH ** -0.5,
        'b_glu': 0.01 * nrm(ks[13], (DEPTH, SSM_WIDTH), jnp.float32),
        'attn_sink': 0.5 * nrm(ks[14], (DEPTH, N_Q_HEADS), jnp.float32),
        'ssm_out_norm': gain(ks[15], (DEPTH, SSM_WIDTH)),
        'attn_out_norm': gain(ks[16], (DEPTH, ATTN_WIDTH)),
        'w_out': nrm(ks[17], (DEPTH, D_MODEL, D_MODEL), jnp.float32) * D_MODEL ** -0.5,
        'post_mix_norm': gain(ks[18], (DEPTH, D_MODEL)),
        'pre_ffn_norm': gain(ks[19], (DEPTH, D_MODEL)),
        'w_up': nrm(ks[20], (DEPTH, D_MODEL, 2 * D_FF), jnp.float32) * D_MODEL ** -0.5,
        'conv_w': nrm(ks[21], (DEPTH, CONV_WIDTH, 2 * D_FF), jnp.float32) * CONV_WIDTH ** -0.5,
        'conv_b': 0.01 * nrm(ks[22], (DEPTH, 2 * D_FF), jnp.float32),
        'w_down': nrm(ks[23], (DEPTH, D_FF, D_MODEL), jnp.float32) * D_FF ** -0.5,
        'post_ffn_norm': gain(ks[24], (DEPTH, D_MODEL)),
    }


def reference(x, rel_bias, pre_mix_norm, w_in, lam_re, lam_im, log_step, b_re, b_im, c_re, c_im,
              ssm_d, w_glu, b_glu, attn_sink, ssm_out_norm, attn_out_norm, w_out, post_mix_norm,
              pre_ffn_norm, w_up, conv_w, conv_b, w_down, post_ffn_norm):
    for l in range(DEPTH):
        h = rms_norm(x, pre_mix_norm[l])
        proj = h @ w_in[l]
        u = proj[..., :SSM_WIDTH]
        q = proj[..., SSM_WIDTH:SSM_WIDTH + ATTN_WIDTH]
        k = proj[..., SSM_WIDTH + ATTN_WIDTH:SSM_WIDTH + ATTN_WIDTH + KV_WIDTH]
        v = proj[..., SSM_WIDTH + ATTN_WIDTH + KV_WIDTH:]
        y_ssm = s5_mixer(u, lam_re[l], lam_im[l], log_step[l], b_re[l], b_im[l], c_re[l], c_im[l],
                         ssm_d[l], w_glu[l], b_glu[l])
        y_att = banded_attention(q, k, v, attn_sink[l], rel_bias)
        merged = jnp.concatenate([rms_norm(y_ssm, ssm_out_norm[l]),
                                  rms_norm(y_att, attn_out_norm[l])], axis=-1) @ w_out[l]
        x = x + rms_norm(merged, post_mix_norm[l])
        h = rms_norm(x, pre_ffn_norm[l])
        x = x + rms_norm(conv_ffn(h, w_up[l], conv_w[l], conv_b[l], w_down[l]), post_ffn_norm[l])
    return x
```

```python
import functools
import math

import numpy as np
import jax
import jax.numpy as jnp
from jax import lax
from jax.experimental import pallas as pl
from jax.experimental.pallas import tpu as pltpu

F32 = jnp.float32
BF16 = jnp.bfloat16

SSM_GROUP = 16
SSM_STATE = 64
HEAD_DIM = 64
N_KV_HEADS = 2
WINDOW = 128
BLOCK = 128
N_BUCKETS = 32
MAX_DISTANCE = 128
EPS = 1e-6
NEG_INF = -1e30
SSM_CHUNK = 16

VMEM_LIMIT_BYTES = 56 * 1024 * 1024


def _params(*sem):
    return pltpu.CompilerParams(dimension_semantics=sem, vmem_limit_bytes=VMEM_LIMIT_BYTES)


def _rms(x, g):
    return x * lax.rsqrt(jnp.mean(x * x, axis=-1, keepdims=True) + EPS) * g


def _gelu(x):
    c = math.sqrt(2.0 / math.pi)
    return 0.5 * x * (1.0 + jnp.tanh(c * (x + 0.044715 * (x * x * x))))


def _bucket_table():
    half = N_BUCKETS // 2
    max_exact = half // 2
    qi = np.arange(BLOCK)[:, None]
    sj = np.arange(3 * BLOCK)[None, :]
    rel = sj - BLOCK - qi
    n = np.abs(rel)
    nf = np.maximum(n, 1).astype(np.float64)
    large = max_exact + (np.log(nf / max_exact) / math.log(MAX_DISTANCE / max_exact)
                         * (half - max_exact)).astype(np.int32)
    large = np.minimum(large, half - 1)
    bucket = np.where(rel > 0, half, 0) + np.where(n < max_exact, n, large)
    return np.where(n <= WINDOW, bucket, -1).astype(np.int32)


def _bias_kernel(bucket_ref, rel_bias_ref, out_ref):
    bucket = bucket_ref[...]
    for h in range(out_ref.shape[0]):
        acc = jnp.full(bucket.shape, NEG_INF, F32)
        for b in range(N_BUCKETS):
            acc = jnp.where(bucket == b, rel_bias_ref[b, h], acc)
        out_ref[h] = acc


def _bias_table(rel_bias):
    n_heads = rel_bias.shape[1]
    bucket = jnp.asarray(_bucket_table())
    return pl.pallas_call(
        _bias_kernel,
        out_shape=jax.ShapeDtypeStruct((n_heads, BLOCK, 3 * BLOCK), F32),
        in_specs=[pl.BlockSpec(memory_space=pltpu.VMEM), pl.BlockSpec(memory_space=pltpu.SMEM)],
        out_specs=pl.BlockSpec(memory_space=pltpu.VMEM),
        name="bias_table",
    )(bucket, rel_bias.astype(F32))


def _proj_kernel(x_ref, g_ref, w_ref, o_ref):
    h = _rms(x_ref[...], g_ref[...]).astype(BF16)
    o_ref[...] = jnp.dot(h, w_ref[...], preferred_element_type=F32).astype(BF16)


def _proj(x2d, g, w, tm):
    t, d = x2d.shape
    n = w.shape[1]
    return pl.pallas_call(
        _proj_kernel,
        out_shape=jax.ShapeDtypeStruct((t, n), BF16),
        grid=(t // tm,),
        in_specs=[pl.BlockSpec((tm, d), lambda i: (i, 0)),
                  pl.BlockSpec((1, d), lambda i: (0, 0)),
                  pl.BlockSpec((d, n), lambda i: (0, 0))],
        out_specs=pl.BlockSpec((tm, n), lambda i: (i, 0)),
        compiler_params=_params("parallel"),
        name="proj",
    )(x2d, g, w)


def _attn_kernel(q_ref, kp_ref, kc_ref, kn_ref, vp_ref, vc_ref, vn_ref, bias_ref, sink_ref, o_ref):
    n = pl.program_id(1)
    nb = pl.num_programs(1)
    k = jnp.concatenate([kp_ref[0], kc_ref[0], kn_ref[0]], axis=0)
    v = jnp.concatenate([vp_ref[0], vc_ref[0], vn_ref[0]], axis=0)
    col = lax.broadcasted_iota(jnp.int32, (BLOCK, 3 * BLOCK), 1)
    lo = jnp.where(n == 0, BLOCK, 0)
    hi = jnp.where(n == nb - 1, 2 * BLOCK, 3 * BLOCK)
    in_seq = (col >= lo) & (col < hi)
    n_heads = bias_ref.shape[0]
    q_per_kv = n_heads // N_KV_HEADS
    outs = []
    for h in range(n_heads):
        kk = h // q_per_kv
        qh = q_ref[0, :, h * HEAD_DIM:(h + 1) * HEAD_DIM]
        kh = k[:, kk * HEAD_DIM:(kk + 1) * HEAD_DIM]
        vh = v[:, kk * HEAD_DIM:(kk + 1) * HEAD_DIM]
        s = lax.dot_general(qh, kh, (((1,), (1,)), ((), ())), preferred_element_type=F32)
        s = jnp.where(in_seq, s + bias_ref[h], NEG_INF)
        sink = sink_ref[h]
        m = jnp.maximum(jnp.max(s, axis=-1, keepdims=True), sink)
        e = jnp.exp(s - m)
        denom = jnp.sum(e, axis=-1, keepdims=True) + jnp.exp(sink - m)
        o = jnp.dot(e.astype(BF16), vh, preferred_element_type=F32)
        outs.append(o / denom)
    o_ref[0] = jnp.concatenate(outs, axis=1).astype(BF16)


def _attention(proj3, bias, sink, ssm_width, attn_width):
    bsz, seq, _ = proj3.shape
    nb = seq // BLOCK
    kv_width = N_KV_HEADS * HEAD_DIM
    q_col = ssm_width // attn_width
    k_col = (ssm_width + attn_width) // kv_width
    v_col = k_col + 1
    kv_spec = lambda col, off: pl.BlockSpec(
        (1, BLOCK, kv_width), lambda b, n: (b, jnp.clip(n + off, 0, nb - 1), col))
    return pl.pallas_call(
        _attn_kernel,
        out_shape=jax.ShapeDtypeStruct((bsz, seq, attn_width), BF16),
        grid=(bsz, nb),
        in_specs=[pl.BlockSpec((1, BLOCK, attn_width), lambda b, n: (b, n, q_col)),
                  kv_spec(k_col, -1), kv_spec(k_col, 0), kv_spec(k_col, 1),
                  kv_spec(v_col, -1), kv_spec(v_col, 0), kv_spec(v_col, 1),
                  pl.BlockSpec(bias.shape, lambda b, n: (0, 0, 0)),
                  pl.BlockSpec(memory_space=pltpu.SMEM)],
        out_specs=pl.BlockSpec((1, BLOCK, attn_width), lambda b, n: (b, n, 0)),
        compiler_params=_params("parallel", "parallel"),
        name="attention",
    )(proj3, proj3, proj3, proj3, proj3, proj3, proj3, bias, sink)


def _ssm_operators(lam_re, lam_im, log_step, b_re, b_im, c_re, c_im, d):
    hp = lax.Precision.HIGHEST
    t_len = SSM_CHUNK
    lam_re, lam_im = lam_re.astype(F32), lam_im.astype(F32)
    dt = jnp.exp(log_step.astype(F32))[..., None]
    mag = jnp.exp(lam_re * dt)
    ang = lam_im * dt
    lb_re, lb_im = mag * jnp.cos(ang), mag * jnp.sin(ang)
    den = lam_re * lam_re + lam_im * lam_im
    nr, ni = lb_re - 1.0, lb_im
    coef_re = (nr * lam_re + ni * lam_im) / den
    coef_im = (ni * lam_re - nr * lam_im) / den
    b_re, b_im = b_re.astype(F32), b_im.astype(F32)
    bb_re = coef_re[..., None] * b_re - coef_im[..., None] * b_im
    bb_im = coef_re[..., None] * b_im + coef_im[..., None] * b_re
    c_re, c_im = c_re.astype(F32), c_im.astype(F32)

    tau = jnp.arange(t_len + 1, dtype=F32)
    pw_mag = jnp.exp((lam_re * dt)[..., None] * tau)
    pw_ang = (lam_im * dt)[..., None] * tau
    pw_re, pw_im = pw_mag * jnp.cos(pw_ang), pw_mag * jnp.sin(pw_ang)

    cp_re = c_re[..., None] * pw_re[:, :, None] - c_im[..., None] * pw_im[:, :, None]
    cp_im = c_re[..., None] * pw_im[:, :, None] + c_im[..., None] * pw_re[:, :, None]
    k_tau = (jnp.einsum('dgpnt,dgnq->dgtpq', cp_re, bb_re, precision=hp)
             - jnp.einsum('dgpnt,dgnq->dgtpq', cp_im, bb_im, precision=hp))

    g_cnt, p_cnt = d.shape[0] // SSM_GROUP, SSM_GROUP
    s_idx = jnp.arange(t_len)[:, None]
    t_idx = jnp.arange(t_len)[None, :]
    lag_f = jnp.clip(t_idx - s_idx, 0, t_len)
    lag_b = jnp.clip(s_idx - t_idx, 0, t_len)
    kf = jnp.where((t_idx >= s_idx)[None, :, :, None, None], k_tau[0][:, lag_f], 0.0)
    kb = jnp.where((s_idx >= t_idx)[None, :, :, None, None], k_tau[1][:, lag_b], 0.0)
    d_mat = (d.astype(F32).reshape(g_cnt, 1, 1, p_cnt, 1)
             * jnp.eye(p_cnt, dtype=F32)[None, None, None]
             * jnp.eye(t_len, dtype=F32)[None, :, :, None, None])
    kbig = (kf + kb + d_mat).transpose(0, 1, 4, 2, 3).reshape(g_cnt, t_len * p_cnt, t_len * p_cnt)

    def lift_b(direction, power_idx):
        pr = pw_re[direction][:, :, power_idx]
        pi = pw_im[direction][:, :, power_idx]
        re = pr[..., None] * bb_re[direction][:, :, None] - pi[..., None] * bb_im[direction][:, :, None]
        im = pr[..., None] * bb_im[direction][:, :, None] + pi[..., None] * bb_re[direction][:, :, None]
        to_rows = lambda m: m.transpose(0, 2, 3, 1).reshape(g_cnt, t_len * p_cnt, SSM_STATE)
        return to_rows(re), to_rows(im)

    steps = jnp.arange(t_len)
    bf_re, bf_im = lift_b(0, t_len - 1 - steps)
    bb_re_l, bb_im_l = lift_b(1, steps)
    bbig = jnp.concatenate([bf_re, bb_re_l, bf_im, bb_im_l], axis=-1)

    def lift_c(direction, power_idx):
        cr = cp_re[direction][..., power_idx]
        ci = cp_im[direction][..., power_idx]
        to_cols = lambda m: m.transpose(0, 2, 3, 1).reshape(g_cnt, SSM_STATE, t_len * p_cnt)
        return to_cols(cr), to_cols(ci)

    cf_re, cf_im = lift_c(0, steps + 1)
    cb_re, cb_im = lift_c(1, t_len - steps)
    cbig = jnp.concatenate([cf_re, cb_re, -cf_im, -cb_im], axis=1)

    a_re = jnp.concatenate([pw_re[0][..., t_len], pw_re[1][..., t_len]], axis=-1)
    a_im = jnp.concatenate([pw_im[0][..., t_len], pw_im[1][..., t_len]], axis=-1)
    a16 = jnp.stack([a_re, a_im], axis=1)
    return bbig.astype(BF16), kbig.astype(BF16), cbig.astype(BF16), a16


def _ssm_kernel(u_ref, bbig_ref, kbig_ref, cbig_ref, a_ref, z_ref, v_scr, s_scr, sp_scr, *, bsz, rows_per_dot):
    rows = u_ref.shape[1]
    nc = rows // bsz
    half = 2 * SSM_STATE

    def local_states(i, carry):
        r = pl.multiple_of(i * rows_per_dot, rows_per_dot)
        v_scr[pl.ds(r, rows_per_dot), :] = jnp.dot(
            u_ref[0, pl.ds(r, rows_per_dot), :], bbig_ref[0], preferred_element_type=F32)
        return carry

    lax.fori_loop(0, rows // rows_per_dot, local_states, 0)

    a_re = a_ref[0, 0:1, :]
    a_im = a_ref[0, 1:2, :]
    is_fwd = lax.broadcasted_iota(jnp.int32, (bsz, half), 1) < SSM_STATE

    def scan_step(i, carry):
        s_re, s_im = carry
        rf = pl.multiple_of(i * bsz, bsz)
        rb = pl.multiple_of((nc - 1 - i) * bsz, bsz)
        s_scr[pl.ds(rf, bsz), 0:half] = s_re
        s_scr[pl.ds(rf, bsz), half:2 * half] = s_im
        v_re = jnp.where(is_fwd, v_scr[pl.ds(rf, bsz), 0:half], v_scr[pl.ds(rb, bsz), 0:half])
        v_im = jnp.where(is_fwd, v_scr[pl.ds(rf, bsz), half:2 * half], v_scr[pl.ds(rb, bsz), half:2 * half])
        n_re = a_re * s_re - a_im * s_im + v_re
        n_im = a_re * s_im + a_im * s_re + v_im
        return n_re, n_im

    zero = jnp.zeros((bsz, half), F32)
    lax.fori_loop(0, nc, scan_step, (zero, zero))

    is_fwd2 = lax.broadcasted_iota(jnp.int32, (bsz, 2 * half), 1) % half < SSM_STATE

    def entering_states(c, carry):
        rf = pl.multiple_of(c * bsz, bsz)
        rb = pl.multiple_of((nc - 1 - c) * bsz, bsz)
        sp_scr[pl.ds(rf, bsz), :] = jnp.where(
            is_fwd2, s_scr[pl.ds(rf, bsz), :], s_scr[pl.ds(rb, bsz), :]).astype(BF16)
        return carry

    lax.fori_loop(0, nc, entering_states, 0)

    def outputs(i, carry):
        r = pl.multiple_of(i * rows_per_dot, rows_per_dot)
        y = (jnp.dot(u_ref[0, pl.ds(r, rows_per_dot), :], kbig_ref[0], preferred_element_type=F32)
             + jnp.dot(sp_scr[pl.ds(r, rows_per_dot), :], cbig_ref[0], preferred_element_type=F32))
        z_ref[0, pl.ds(r, rows_per_dot), :] = _gelu(y).astype(BF16)
        return carry

    lax.fori_loop(0, rows // rows_per_dot, outputs, 0)


def _ssm(u_t, bbig, kbig, cbig, a16, bsz):
    g_cnt, rows, width = u_t.shape
    rows_per_dot = min(rows, 512)
    n_state = bbig.shape[-1]
    return pl.pallas_call(
        functools.partial(_ssm_kernel, bsz=bsz, rows_per_dot=rows_per_dot),
        out_shape=jax.ShapeDtypeStruct((g_cnt, rows, width), BF16),
        grid=(g_cnt,),
        in_specs=[pl.BlockSpec((1, rows, width), lambda g: (g, 0, 0)),
                  pl.BlockSpec((1, width, n_state), lambda g: (g, 0, 0)),
                  pl.BlockSpec((1, width, width), lambda g: (g, 0, 0)),
                  pl.BlockSpec((1, n_state, width), lambda g: (g, 0, 0)),
                  pl.BlockSpec((1, 2, n_state // 2), lambda g: (g, 0, 0))],
        out_specs=pl.BlockSpec((1, rows, width), lambda g: (g, 0, 0)),
        scratch_shapes=[pltpu.VMEM((rows, n_state), F32),
                        pltpu.VMEM((rows, n_state), F32),
                        pltpu.VMEM((rows, n_state), BF16)],
        compiler_params=_params("parallel"),
        name="s5_mixer",
    )(u_t, bbig, kbig, cbig, a16)


def _merge_kernel(z_ref, att_ref, x_ref, wglu_ref, bglu_ref, gssm_ref, gatt_ref, wout_ref, gpost_ref, o_ref):
    z = z_ref[...]
    gate = jnp.dot(z, wglu_ref[...], preferred_element_type=F32) + bglu_ref[...]
    y_ssm = z.astype(F32) * jax.nn.sigmoid(gate)
    y_att = att_ref[...].astype(F32)
    merged = jnp.concatenate([_rms(y_ssm, gssm_ref[...]).astype(BF16),
                              _rms(y_att, gatt_ref[...]).astype(BF16)], axis=-1)
    y = jnp.dot(merged, wout_ref[...], preferred_element_type=F32)
    o_ref[...] = x_ref[...] + _rms(y, gpost_ref[...])


def _merge(z2d, att2d, x2d, w_glu, b_glu, g_ssm, g_att, w_out, g_post, tm):
    t, d = x2d.shape
    ws, wa = z2d.shape[1], att2d.shape[1]
    row = lambda w: pl.BlockSpec((tm, w), lambda i: (i, 0))
    full = lambda a: pl.BlockSpec(a.shape, lambda i: (0,) * a.ndim)
    return pl.pallas_call(
        _merge_kernel,
        out_shape=jax.ShapeDtypeStruct((t, d), F32),
        grid=(t // tm,),
        in_specs=[row(ws), row(wa), row(d), full(w_glu), full(b_glu), full(g_ssm), full(g_att),
                  full(w_out), full(g_post)],
        out_specs=row(d),
        compiler_params=_params("parallel"),
        name="merge",
    )(z2d, att2d, x2d, w_glu, b_glu, g_ssm, g_att, w_out, g_post)


FFN_HALO = 16


def _ffn_kernel(xp_ref, x_ref, xn_ref, gpre_ref, wup_ref, cw_ref, cb_ref, wdown_ref, gpost_ref, o_ref,
                h_scr, acc_scr, *, d_ff, ff_chunk):
    j = pl.program_id(1)
    nj = pl.num_programs(1)
    tm = x_ref.shape[1]
    g = gpre_ref[...]
    x = x_ref[0]
    keep_prev = jnp.where(j == 0, 0.0, 1.0)
    keep_next = jnp.where(j == nj - 1, 0.0, 1.0)
    h_scr[0:FFN_HALO, :] = (_rms(xp_ref[0], g) * keep_prev).astype(BF16)
    h_scr[FFN_HALO:FFN_HALO + tm, :] = _rms(x, g).astype(BF16)
    h_scr[FFN_HALO + tm:, :] = (_rms(xn_ref[0], g) * keep_next).astype(BF16)
    acc_scr[...] = jnp.zeros_like(acc_scr)

    def conv(a, col):
        w = cw_ref[:, pl.ds(col, ff_chunk)]
        b = cb_ref[:, pl.ds(col, ff_chunk)]
        return (a[FFN_HALO - 1:FFN_HALO - 1 + tm] * w[0:1] + a[FFN_HALO:FFN_HALO + tm] * w[1:2]
                + a[FFN_HALO + 1:FFN_HALO + 1 + tm] * w[2:3] + b)

    def chunk(c, carry):
        cv = pl.multiple_of(c * ff_chunk, ff_chunk)
        cg = pl.multiple_of(d_ff + c * ff_chunk, ff_chunk)
        h = h_scr[...]
        a_v = jnp.dot(h, wup_ref[:, pl.ds(cv, ff_chunk)], preferred_element_type=F32)
        a_g = jnp.dot(h, wup_ref[:, pl.ds(cg, ff_chunk)], preferred_element_type=F32)
        act = (_gelu(conv(a_g, cg)) * conv(a_v, cv)).astype(BF16)
        acc_scr[...] += jnp.dot(act, wdown_ref[pl.ds(cv, ff_chunk), :], preferred_element_type=F32)
        return carry

    lax.fori_loop(0, d_ff // ff_chunk, chunk, 0)
    o_ref[0] = x + _rms(acc_scr[...], gpost_ref[...])


def _ffn(x3, g_pre, w_up, conv_w, conv_b, w_down, g_post, tm, ff_chunk):
    bsz, seq, d = x3.shape
    d_ff = w_down.shape[0]
    nj = seq // tm
    hb = tm // FFN_HALO
    n_halo_blocks = seq // FFN_HALO
    full = lambda a: pl.BlockSpec(a.shape, lambda b, j: (0,) * a.ndim)
    return pl.pallas_call(
        functools.partial(_ffn_kernel, d_ff=d_ff, ff_chunk=ff_chunk),
        out_shape=jax.ShapeDtypeStruct((bsz, seq, d), F32),
        grid=(bsz, nj),
        in_specs=[pl.BlockSpec((1, FFN_HALO, d), lambda b, j: (b, jnp.maximum(j * hb - 1, 0), 0)),
                  pl.BlockSpec((1, tm, d), lambda b, j: (b, j, 0)),
                  pl.BlockSpec((1, FFN_HALO, d), lambda b, j: (b, jnp.minimum((j + 1) * hb, n_halo_blocks - 1), 0)),
                  full(g_pre), full(w_up), full(conv_w), full(conv_b), full(w_down), full(g_post)],
        out_specs=pl.BlockSpec((1, tm, d), lambda b, j: (b, j, 0)),
        scratch_shapes=[pltpu.VMEM((tm + 2 * FFN_HALO, d), BF16), pltpu.VMEM((tm, d), F32)],
        compiler_params=_params("parallel", "parallel"),
        name="conv_ffn",
    )(x3, x3, x3, g_pre, w_up, conv_w, conv_b, w_down, g_post)


def _tile_sizes(bsz, seq):
    tokens = bsz * seq
    tm_tok = 512 if tokens % 512 == 0 else BLOCK
    tm_ffn = 512 if seq % 512 == 0 else BLOCK
    return tm_tok, tm_ffn, 256


def kernel(x, rel_bias, pre_mix_norm, w_in, lam_re, lam_im, log_step, b_re, b_im, c_re, c_im, ssm_d, w_glu,
           b_glu, attn_sink, ssm_out_norm, attn_out_norm, w_out, post_mix_norm, pre_ffn_norm, w_up, conv_w,
           conv_b, w_down, post_ffn_norm):
    bsz, seq, d_model = x.shape
    depth = w_in.shape[0]
    ssm_width = ssm_d.shape[1]
    attn_width = w_out.shape[1] - ssm_width
    g_cnt = ssm_width // SSM_GROUP
    nc = seq // SSM_CHUNK
    lanes = SSM_CHUNK * SSM_GROUP
    assert seq % BLOCK == 0 and ssm_width % attn_width == 0 and attn_width % (N_KV_HEADS * HEAD_DIM) == 0
    tm_tok, tm_ffn, ff_chunk = _tile_sizes(bsz, seq)
    assert w_down.shape[1] % ff_chunk == 0

    bias = _bias_table(rel_bias)
    row = lambda a: a.astype(F32).reshape(1, -1)
    q_scale = jnp.concatenate([jnp.ones((ssm_width,), F32), jnp.full((attn_width,), HEAD_DIM ** -0.5, F32),
                               jnp.ones((w_in.shape[2] - ssm_width - attn_width,), F32)])

    for l in range(depth):
        x2d = x.reshape(bsz * seq, d_model)
        proj = _proj(x2d, row(pre_mix_norm[l]), (w_in[l] * q_scale).astype(BF16), tm_tok)
        proj3 = proj.reshape(bsz, seq, -1)
        att = _attention(proj3, bias, attn_sink[l].astype(F32), ssm_width, attn_width)

        u_t = (proj3[..., :ssm_width].reshape(bsz, nc, SSM_CHUNK, g_cnt, SSM_GROUP)
               .transpose(3, 1, 0, 2, 4).reshape(g_cnt, nc * bsz, lanes))
        ops = _ssm_operators(lam_re[l], lam_im[l], log_step[l], b_re[l], b_im[l], c_re[l], c_im[l], ssm_d[l])
        z_t = _ssm(u_t, *ops, bsz)
        z = (z_t.reshape(g_cnt, nc, bsz, SSM_CHUNK, SSM_GROUP)
             .transpose(2, 1, 3, 0, 4).reshape(bsz * seq, ssm_width))

        x2d = _merge(z, att.reshape(bsz * seq, attn_width), x2d, w_glu[l].astype(BF16), row(b_glu[l]),
                     row(ssm_out_norm[l]), row(attn_out_norm[l]), w_out[l].astype(BF16),
                     row(post_mix_norm[l]), tm_tok)
        x = _ffn(x2d.reshape(bsz, seq, d_model), row(pre_ffn_norm[l]), w_up[l].astype(BF16),
                 conv_w[l].astype(F32), row(conv_b[l]), w_down[l].astype(BF16), row(post_ffn_norm[l]),
                 tm_ffn, ff_chunk)
    return x
```

```python
import functools
import math

import numpy as np
import jax
import jax.numpy as jnp
from jax import lax
from jax.experimental import pallas as pl
from jax.experimental.pallas import tpu as pltpu

F32 = jnp.float32
BF16 = jnp.bfloat16

LANES = 128
SSM_GROUP = 16
SSM_STATE = 64
HEAD_DIM = 64
N_KV_HEADS = 2
WINDOW = 128
BLOCK = 128
N_BUCKETS = 32
MAX_DISTANCE = 128
EPS = 1e-6
NEG_INF = -1e30
SSM_CHUNK = 16
GROUPS_PER_VREG = LANES // SSM_GROUP

VMEM_LIMIT_BYTES = 56 * 1024 * 1024

HEAD_ORDER = ((0, 2, 5, 7), (1, 3, 4, 6))


def _params(*sem):
    return pltpu.CompilerParams(dimension_semantics=sem, vmem_limit_bytes=VMEM_LIMIT_BYTES)


def _rms(x, g):
    return x * lax.rsqrt(jnp.mean(x * x, axis=-1, keepdims=True) + EPS) * g


def _gelu(x):
    c = math.sqrt(2.0 / math.pi)
    return x * (0.5 + 0.5 * jnp.tanh(x * (c + (c * 0.044715) * (x * x))))


def _atom_transpose(vs):
    atom = lax.broadcasted_iota(jnp.int32, vs[0].shape, 1) // SSM_GROUP
    for d in (4, 2, 1):
        keep = (atom & d) == 0
        nxt = list(vs)
        for i in range(GROUPS_PER_VREG):
            if i & d == 0:
                a, b = vs[i], vs[i + d]
                nxt[i] = jnp.where(keep, a, pltpu.roll(b, SSM_GROUP * d, axis=1))
                nxt[i + d] = jnp.where(keep, pltpu.roll(a, LANES - SSM_GROUP * d, axis=1), b)
        vs = nxt
    return vs


def _bucket_table():
    half = N_BUCKETS // 2
    max_exact = half // 2
    qi = np.arange(BLOCK)[:, None]
    sj = np.arange(3 * BLOCK)[None, :]
    rel = sj - BLOCK - qi
    n = np.abs(rel)
    nf = np.maximum(n, 1).astype(np.float64)
    large = max_exact + (np.log(nf / max_exact) / math.log(MAX_DISTANCE / max_exact)
                         * (half - max_exact)).astype(np.int32)
    large = np.minimum(large, half - 1)
    bucket = np.where(rel > 0, half, 0) + np.where(n < max_exact, n, large)
    return np.where(n <= WINDOW, bucket, -1).astype(np.int32)


def _bias_kernel(bucket_ref, rel_bias_ref, out_ref):
    bucket = bucket_ref[...]
    col = lax.broadcasted_iota(jnp.int32, bucket.shape, 1)
    for grp, heads in enumerate(HEAD_ORDER):
        for slot, h in enumerate(heads):
            acc = jnp.full(bucket.shape, NEG_INF, F32)
            for b in range(N_BUCKETS):
                acc = jnp.where(bucket == b, rel_bias_ref[b, h], acc)
            for variant in range(4):
                lo = BLOCK if variant & 1 else 0
                hi = 2 * BLOCK if variant & 2 else 3 * BLOCK
                out_ref[variant, grp, slot * BLOCK:(slot + 1) * BLOCK, :] = jnp.where(
                    (col >= lo) & (col < hi), acc, NEG_INF)


def _bias_table(rel_bias):
    rows = len(HEAD_ORDER[0]) * BLOCK
    bucket = jnp.asarray(_bucket_table())
    return pl.pallas_call(
        _bias_kernel,
        out_shape=jax.ShapeDtypeStruct((4, len(HEAD_ORDER), rows, 3 * BLOCK), F32),
        in_specs=[pl.BlockSpec(memory_space=pltpu.VMEM), pl.BlockSpec(memory_space=pltpu.SMEM)],
        out_specs=pl.BlockSpec(memory_space=pltpu.VMEM),
        name="bias_table",
    )(bucket, rel_bias.astype(F32))


def _proj_kernel(x_ref, g_ref, w_ref, ut_ref, qkv_ref, *, bsz, ssm_width):
    h = _rms(x_ref[...], g_ref[...]).astype(BF16)
    r = jnp.dot(h, w_ref[...], preferred_element_type=F32)
    qkv_ref[...] = r[:, ssm_width:].astype(BF16)
    for t1 in range(SSM_CHUNK // GROUPS_PER_VREG):
        for g1 in range(ssm_width // LANES):
            vs = [r[(t1 * GROUPS_PER_VREG + t2) * bsz:(t1 * GROUPS_PER_VREG + t2 + 1) * bsz,
                    g1 * LANES:(g1 + 1) * LANES] for t2 in range(GROUPS_PER_VREG)]
            for g2, tile in enumerate(_atom_transpose(vs)):
                ut_ref[g1 * GROUPS_PER_VREG + g2, :, t1 * LANES:(t1 + 1) * LANES] = tile.astype(BF16)


def _proj(x2d, g, w, bsz, ssm_width):
    t, d = x2d.shape
    n = w.shape[1]
    tm = SSM_CHUNK * bsz
    nc = t // tm
    g_cnt = ssm_width // SSM_GROUP
    lanes = SSM_CHUNK * SSM_GROUP
    return pl.pallas_call(
        functools.partial(_proj_kernel, bsz=bsz, ssm_width=ssm_width),
        out_shape=(jax.ShapeDtypeStruct((g_cnt, nc, bsz, lanes), BF16),
                   jax.ShapeDtypeStruct((t, n - ssm_width), BF16)),
        grid=(nc,),
        in_specs=[pl.BlockSpec((tm, d), lambda i: (i, 0)),
                  pl.BlockSpec((1, d), lambda i: (0, 0)),
                  pl.BlockSpec((d, n), lambda i: (0, 0))],
        out_specs=(pl.BlockSpec((g_cnt, None, bsz, lanes), lambda i: (0, i, 0, 0)),
                   pl.BlockSpec((tm, n - ssm_width), lambda i: (i, 0))),
        compiler_params=_params("parallel"),
        name="proj",
    )(x2d, g, w)


def _attn_kernel(q_ref, kp_ref, kc_ref, kn_ref, vp_ref, vc_ref, vn_ref, bias_ref, sink_ref, o_ref):
    n = pl.program_id(1)
    nb = pl.num_programs(1)
    variant = jnp.where(n == 0, 1, 0) + jnp.where(n == nb - 1, 2, 0)
    k = jnp.concatenate([kp_ref[0], kc_ref[0], kn_ref[0]], axis=0)
    v = jnp.concatenate([vp_ref[0], vc_ref[0], vn_ref[0]], axis=0)
    swap = lambda a: jnp.concatenate([a[:, HEAD_DIM:], a[:, :HEAD_DIM]], axis=1)
    k_sw, v_sw = swap(k), swap(v)
    low = lax.broadcasted_iota(jnp.int32, k.shape, 1) < HEAD_DIM
    zero = jnp.zeros_like(k)
    k_even = (jnp.where(low, k, zero), jnp.where(low, k_sw, zero))
    k_odd = (jnp.where(low, zero, k_sw), jnp.where(low, zero, k))
    ones = jnp.ones_like(v)
    v_ext = (jnp.concatenate([v, ones], axis=1), jnp.concatenate([v_sw, ones], axis=1))
    q = q_ref[0]
    q_kv = (jnp.concatenate([q[:, 0:LANES], q[:, LANES:2 * LANES]], axis=0),
            jnp.concatenate([q[:, 2 * LANES:3 * LANES], q[:, 3 * LANES:4 * LANES]], axis=0))
    qk = lambda a, b: lax.dot_general(a, b, (((1,), (1,)), ((), ())), preferred_element_type=F32)
    scores = (jnp.concatenate([qk(q_kv[0], k_even[0]), qk(q_kv[1], k_odd[1])], axis=0),
              jnp.concatenate([qk(q_kv[0], k_odd[0]), qk(q_kv[1], k_even[1])], axis=0))
    outs = []
    for grp, heads in enumerate(HEAD_ORDER):
        sink = jnp.concatenate([jnp.full((BLOCK, 1), sink_ref[h], F32) for h in heads], axis=0)
        s = scores[grp] + bias_ref[variant, grp]
        m = jnp.maximum(jnp.max(s, axis=-1, keepdims=True), sink)
        e = jnp.exp(s - m).astype(BF16)
        o = jnp.dot(e, v_ext[grp], preferred_element_type=F32)
        outs.append(o[:, :LANES] / (o[:, LANES:] + jnp.exp(sink - m)))
    o_a, o_b = outs
    low_o = lax.broadcasted_iota(jnp.int32, (BLOCK, LANES), 1) < HEAD_DIM
    rows = lambda a, j: a[j * BLOCK:(j + 1) * BLOCK]
    pairs = [jnp.where(low_o, rows(o_a, 0), rows(o_b, 0)), jnp.where(low_o, rows(o_a, 1), rows(o_b, 1)),
             jnp.where(low_o, rows(o_b, 2), rows(o_a, 2)), jnp.where(low_o, rows(o_b, 3), rows(o_a, 3))]
    o_ref[0] = jnp.concatenate(pairs, axis=1).astype(BF16)


def _attention(qkv3, bias, sink, attn_width):
    bsz, seq, _ = qkv3.shape
    nb = seq // BLOCK
    kv_width = N_KV_HEADS * HEAD_DIM
    k_col = attn_width // kv_width
    v_col = k_col + 1
    kv_spec = lambda col, off: pl.BlockSpec(
        (1, BLOCK, kv_width), lambda b, n: (b, jnp.clip(n + off, 0, nb - 1), col))
    return pl.pallas_call(
        _attn_kernel,
        out_shape=jax.ShapeDtypeStruct((bsz, seq, attn_width), BF16),
        grid=(bsz, nb),
        in_specs=[pl.BlockSpec((1, BLOCK, attn_width), lambda b, n: (b, n, 0)),
                  kv_spec(k_col, -1), kv_spec(k_col, 0), kv_spec(k_col, 1),
                  kv_spec(v_col, -1), kv_spec(v_col, 0), kv_spec(v_col, 1),
                  pl.BlockSpec(bias.shape, lambda b, n: (0, 0, 0, 0)),
                  pl.BlockSpec(memory_space=pltpu.SMEM)],
        out_specs=pl.BlockSpec((1, BLOCK, attn_width), lambda b, n: (b, n, 0)),
        compiler_params=_params("parallel", "parallel"),
        name="attention",
    )(qkv3, qkv3, qkv3, qkv3, qkv3, qkv3, qkv3, bias, sink)


def _ssm_operators(lam_re, lam_im, log_step, b_re, b_im, c_re, c_im, d):
    hp = lax.Precision.HIGHEST
    t_len = SSM_CHUNK
    lam_re, lam_im = lam_re.astype(F32), lam_im.astype(F32)
    dt = jnp.exp(log_step.astype(F32))[..., None]
    ld_re, ld_im = lam_re * dt, lam_im * dt
    mag = jnp.exp(ld_re)
    lb_re, lb_im = mag * jnp.cos(ld_im), mag * jnp.sin(ld_im)
    den = lam_re * lam_re + lam_im * lam_im
    nr, ni = lb_re - 1.0, lb_im
    coef_re = (nr * lam_re + ni * lam_im) / den
    coef_im = (ni * lam_re - nr * lam_im) / den
    b_re, b_im = b_re.astype(F32), b_im.astype(F32)
    bb_re = coef_re[..., None] * b_re - coef_im[..., None] * b_im
    bb_im = coef_re[..., None] * b_im + coef_im[..., None] * b_re
    g_cnt, n_cnt, p_cnt = bb_re.shape[1:]

    def powers(re, im, tau):
        m = jnp.exp(re * tau)
        return m * jnp.cos(im * tau), m * jnp.sin(im * tau)

    tau = jnp.arange(t_len + 1, dtype=F32)
    pw_re, pw_im = powers(ld_re[..., None], ld_im[..., None], tau)
    pwt_re, pwt_im = powers(ld_re[:, :, None], ld_im[:, :, None], tau[:, None])

    ct_re = c_re.astype(F32).transpose(0, 1, 3, 2)[:, :, :, None, :]
    ct_im = c_im.astype(F32).transpose(0, 1, 3, 2)[:, :, :, None, :]
    cp_re = ct_re * pw_re[..., None] - ct_im * pw_im[..., None]
    cp_im = ct_re * pw_im[..., None] + ct_im * pw_re[..., None]
    k_lag = (jnp.einsum('dgnq,dgntp->dgqtp', bb_re, cp_re[:, :, :, :t_len], precision=hp)
             - jnp.einsum('dgnq,dgntp->dgqtp', bb_im, cp_im[:, :, :, :t_len], precision=hp))
    skip = d.astype(F32).reshape(g_cnt, 1, p_cnt) * jnp.eye(p_cnt, dtype=F32)
    k_fwd = k_lag[0].at[:, :, 0].add(skip)
    zeros = jnp.zeros_like(k_fwd)
    f_pad = jnp.concatenate([zeros, k_fwd], axis=2)
    b_pad = jnp.concatenate([k_lag[1][:, :, ::-1], zeros], axis=2)
    kbig = jnp.stack([f_pad[:, :, t_len - s:2 * t_len - s] + b_pad[:, :, t_len - 1 - s:2 * t_len - 1 - s]
                      for s in range(t_len)], axis=1)
    kbig = kbig.reshape(g_cnt, t_len * p_cnt, t_len * p_cnt)

    def lift_b(direction, pr, pi):
        br = bb_re[direction].transpose(0, 2, 1)[:, None]
        bi = bb_im[direction].transpose(0, 2, 1)[:, None]
        pr, pi = pr[:, :, None], pi[:, :, None]
        return pr * br - pi * bi, pr * bi + pi * br

    bf_re, bf_im = lift_b(0, pwt_re[0][:, :t_len][:, ::-1], pwt_im[0][:, :t_len][:, ::-1])
    bw_re, bw_im = lift_b(1, pwt_re[1][:, :t_len], pwt_im[1][:, :t_len])
    bbig = jnp.concatenate([bf_re, bw_re, bf_im, bw_im], axis=-1).reshape(g_cnt, t_len * p_cnt, 4 * n_cnt)

    fwd = lambda m: m[0][:, :, 1:t_len + 1]
    bwd = lambda m: m[1][:, :, 1:t_len + 1][:, :, ::-1]
    cbig = jnp.concatenate([fwd(cp_re), bwd(cp_re), -fwd(cp_im), -bwd(cp_im)], axis=1)
    cbig = cbig.reshape(g_cnt, 4 * n_cnt, t_len * p_cnt)

    a_re = jnp.concatenate([pw_re[0][..., t_len], pw_re[1][..., t_len]], axis=-1)
    a_im = jnp.concatenate([pw_im[0][..., t_len], pw_im[1][..., t_len]], axis=-1)
    a16 = jnp.stack([a_re, a_im], axis=1)
    return bbig.astype(BF16), kbig.astype(BF16), cbig.astype(BF16), a16


def _ssm_kernel(u_ref, bbig_ref, kbig_ref, cbig_ref, a_ref, z_ref, v_scr, s_scr, sp_scr, *, bsz, rows_per_dot):
    rows = u_ref.shape[1]
    nc = rows // bsz
    half = 2 * SSM_STATE

    def local_states(i, carry):
        r = pl.multiple_of(i * rows_per_dot, rows_per_dot)
        v_scr[pl.ds(r, rows_per_dot), :] = jnp.dot(
            u_ref[0, pl.ds(r, rows_per_dot), :], bbig_ref[0], preferred_element_type=F32)
        return carry

    lax.fori_loop(0, rows // rows_per_dot, local_states, 0)

    a_re = a_ref[0, 0:1, :]
    a_im = a_ref[0, 1:2, :]
    is_fwd = lax.broadcasted_iota(jnp.int32, (bsz, half), 1) < SSM_STATE

    def scan_step(i, carry):
        s_re, s_im = carry
        rf = pl.multiple_of(i * bsz, bsz)
        rb = pl.multiple_of((nc - 1 - i) * bsz, bsz)
        s_scr[pl.ds(rf, bsz), 0:half] = s_re
        s_scr[pl.ds(rf, bsz), half:2 * half] = s_im
        v_re = jnp.where(is_fwd, v_scr[pl.ds(rf, bsz), 0:half], v_scr[pl.ds(rb, bsz), 0:half])
        v_im = jnp.where(is_fwd, v_scr[pl.ds(rf, bsz), half:2 * half], v_scr[pl.ds(rb, bsz), half:2 * half])
        n_re = a_re * s_re - a_im * s_im + v_re
        n_im = a_re * s_im + a_im * s_re + v_im
        return n_re, n_im

    zero = jnp.zeros((bsz, half), F32)
    lax.fori_loop(0, nc, scan_step, (zero, zero))

    is_fwd2 = lax.broadcasted_iota(jnp.int32, (bsz, 2 * half), 1) % half < SSM_STATE

    def entering_states(c, carry):
        rf = pl.multiple_of(c * bsz, bsz)
        rb = pl.multiple_of((nc - 1 - c) * bsz, bsz)
        sp_scr[pl.ds(rf, bsz), :] = jnp.where(
            is_fwd2, s_scr[pl.ds(rf, bsz), :], s_scr[pl.ds(rb, bsz), :]).astype(BF16)
        return carry

    lax.fori_loop(0, nc, entering_states, 0)

    def outputs(i, carry):
        r = pl.multiple_of(i * rows_per_dot, rows_per_dot)
        y = (jnp.dot(u_ref[0, pl.ds(r, rows_per_dot), :], kbig_ref[0], preferred_element_type=F32)
             + jnp.dot(sp_scr[pl.ds(r, rows_per_dot), :], cbig_ref[0], preferred_element_type=F32))
        z_ref[0, pl.ds(r, rows_per_dot), :] = _gelu(y).astype(BF16)
        return carry

    lax.fori_loop(0, rows // rows_per_dot, outputs, 0)


def _ssm(u_t, bbig, kbig, cbig, a16, bsz):
    g_cnt, rows, width = u_t.shape
    rows_per_dot = min(rows, 512)
    n_state = bbig.shape[-1]
    return pl.pallas_call(
        functools.partial(_ssm_kernel, bsz=bsz, rows_per_dot=rows_per_dot),
        out_shape=jax.ShapeDtypeStruct((g_cnt, rows, width), BF16),
        grid=(g_cnt,),
        in_specs=[pl.BlockSpec((1, rows, width), lambda g: (g, 0, 0)),
                  pl.BlockSpec((1, width, n_state), lambda g: (g, 0, 0)),
                  pl.BlockSpec((1, width, width), lambda g: (g, 0, 0)),
                  pl.BlockSpec((1, n_state, width), lambda g: (g, 0, 0)),
                  pl.BlockSpec((1, 2, n_state // 2), lambda g: (g, 0, 0))],
        out_specs=pl.BlockSpec((1, rows, width), lambda g: (g, 0, 0)),
        scratch_shapes=[pltpu.VMEM((rows, n_state), F32),
                        pltpu.VMEM((rows, n_state), F32),
                        pltpu.VMEM((rows, n_state), BF16)],
        compiler_params=_params("parallel"),
        name="s5_mixer",
    )(u_t, bbig, kbig, cbig, a16)


def _merge_kernel(zt_ref, att_ref, x_ref, wglu_ref, bglu_ref, gssm_ref, gatt_ref, wout_ref, gpost_ref, o_ref,
                  z_scr, *, bsz):
    for t1 in range(SSM_CHUNK // GROUPS_PER_VREG):
        for g1 in range(z_scr.shape[1] // LANES):
            vs = [zt_ref[g1 * GROUPS_PER_VREG + g2, :, t1 * LANES:(t1 + 1) * LANES].astype(F32)
                  for g2 in range(GROUPS_PER_VREG)]
            for t2, tile in enumerate(_atom_transpose(vs)):
                t = t1 * GROUPS_PER_VREG + t2
                z_scr[t * bsz:(t + 1) * bsz, g1 * LANES:(g1 + 1) * LANES] = tile
    z = z_scr[...]
    gate = jnp.dot(z.astype(BF16), wglu_ref[...], preferred_element_type=F32) + bglu_ref[...]
    y_ssm = z * jax.nn.sigmoid(gate)
    y_att = att_ref[...].astype(F32)
    merged = jnp.concatenate([_rms(y_ssm, gssm_ref[...]).astype(BF16),
                              _rms(y_att, gatt_ref[...]).astype(BF16)], axis=-1)
    y = jnp.dot(merged, wout_ref[...], preferred_element_type=F32)
    o_ref[...] = x_ref[...] + _rms(y, gpost_ref[...])


def _merge(z_t, att2d, x2d, w_glu, b_glu, g_ssm, g_att, w_out, g_post, bsz):
    t, d = x2d.shape
    g_cnt, nc, _, lanes = z_t.shape
    ws, wa = g_cnt * SSM_GROUP, att2d.shape[1]
    tm = SSM_CHUNK * bsz
    row = lambda w: pl.BlockSpec((tm, w), lambda i: (i, 0))
    full = lambda a: pl.BlockSpec(a.shape, lambda i: (0,) * a.ndim)
    return pl.pallas_call(
        functools.partial(_merge_kernel, bsz=bsz),
        out_shape=jax.ShapeDtypeStruct((t, d), F32),
        grid=(nc,),
        in_specs=[pl.BlockSpec((g_cnt, None, bsz, lanes), lambda i: (0, i, 0, 0)),
                  row(wa), row(d), full(w_glu), full(b_glu), full(g_ssm), full(g_att),
                  full(w_out), full(g_post)],
        out_specs=row(d),
        scratch_shapes=[pltpu.VMEM((tm, ws), F32)],
        compiler_params=_params("parallel"),
        name="merge",
    )(z_t, att2d, x2d, w_glu, b_glu, g_ssm, g_att, w_out, g_post)


def _ffn_kernel(xp_ref, x_ref, xn_ref, gpre_ref, wup_ref, cw_ref, cb_ref, wdown_ref, gpost_ref, o_ref,
                h_scr, a_scr, act_scr, *, d_ff, ff_chunk, halo):
    i = pl.program_id(0)
    n_tiles = pl.num_programs(0)
    tm = x_ref.shape[0]
    g = gpre_ref[...]
    keep_prev = jnp.where(i == 0, 0.0, 1.0)
    keep_next = jnp.where(i == n_tiles - 1, 0.0, 1.0)
    h_scr[0:halo, :] = (_rms(xp_ref[...], g) * keep_prev).astype(BF16)
    h_scr[halo:halo + tm, :] = _rms(x_ref[...], g).astype(BF16)
    h_scr[halo + tm:, :] = (_rms(xn_ref[...], g) * keep_next).astype(BF16)
    n_chunks = d_ff // ff_chunk

    def cols(base, c):
        start = base + c * ff_chunk
        return pl.ds(start if isinstance(start, int) else pl.multiple_of(start, ff_chunk), ff_chunk)

    def up(c, slot):
        h = h_scr[...]
        for part, base in enumerate((0, d_ff)):
            a_scr[slot, part] = jnp.dot(h, wup_ref[:, cols(base, c)], preferred_element_type=F32)

    def post(c, slot):
        def conv(part, base):
            w = cw_ref[:, cols(base, c)]
            return (a_scr[slot, part, 0:tm] * w[0:1] + a_scr[slot, part, halo:halo + tm] * w[1:2]
                    + a_scr[slot, part, 2 * halo:2 * halo + tm] * w[2:3] + cb_ref[:, cols(base, c)])

        act = _gelu(conv(1, d_ff)) * conv(0, 0)
        act_scr[:, cols(0, c)] = act.astype(BF16)

    up(0, 0)

    def pair(p, carry):
        c = 2 * p
        up(c + 1, 1)
        post(c, 0)
        up(c + 2, 0)
        post(c + 1, 1)
        return carry

    lax.fori_loop(0, (n_chunks - 1) // 2, pair, 0)
    if n_chunks % 2 == 0:
        up(n_chunks - 1, 1)
        post(n_chunks - 2, 0)
        post(n_chunks - 1, 1)
    else:
        post(n_chunks - 1, 0)

    y = jnp.dot(act_scr[...], wdown_ref[...], preferred_element_type=F32)
    o_ref[...] = x_ref[...] + _rms(y, gpost_ref[...])


def _ffn(x2d, g_pre, w_up, conv_w, conv_b, w_down, g_post, bsz, tm, ff_chunk):
    t, d = x2d.shape
    d_ff = w_down.shape[0]
    halo = bsz
    hb = tm // halo
    n_halo_blocks = t // halo
    full = lambda a: pl.BlockSpec(a.shape, lambda i: (0,) * a.ndim)
    return pl.pallas_call(
        functools.partial(_ffn_kernel, d_ff=d_ff, ff_chunk=ff_chunk, halo=halo),
        out_shape=jax.ShapeDtypeStruct((t, d), F32),
        grid=(t // tm,),
        in_specs=[pl.BlockSpec((halo, d), lambda i: (jnp.maximum(i * hb - 1, 0), 0)),
                  pl.BlockSpec((tm, d), lambda i: (i, 0)),
                  pl.BlockSpec((halo, d), lambda i: (jnp.minimum((i + 1) * hb, n_halo_blocks - 1), 0)),
                  full(g_pre), full(w_up), full(conv_w), full(conv_b), full(w_down), full(g_post)],
        out_specs=pl.BlockSpec((tm, d), lambda i: (i, 0)),
        scratch_shapes=[pltpu.VMEM((tm + 2 * halo, d), BF16),
                        pltpu.VMEM((2, 2, tm + 2 * halo, ff_chunk), F32),
                        pltpu.VMEM((tm, d_ff), BF16)],
        compiler_params=_params("parallel"),
        name="conv_ffn",
    )(x2d, x2d, x2d, g_pre, w_up, conv_w, conv_b, w_down, g_post)


def _tile_sizes(bsz, seq):
    steps = 16 if seq % 16 == 0 else 1
    return steps * bsz, 256


def kernel(x, rel_bias, pre_mix_norm, w_in, lam_re, lam_im, log_step, b_re, b_im, c_re, c_im, ssm_d, w_glu,
           b_glu, attn_sink, ssm_out_norm, attn_out_norm, w_out, post_mix_norm, pre_ffn_norm, w_up, conv_w,
           conv_b, w_down, post_ffn_norm):
    bsz, seq, d_model = x.shape
    depth = w_in.shape[0]
    ssm_width = ssm_d.shape[1]
    attn_width = w_out.shape[1] - ssm_width
    g_cnt = ssm_width // SSM_GROUP
    nc = seq // SSM_CHUNK
    assert seq % BLOCK == 0 and bsz % 16 == 0 and ssm_width % LANES == 0
    assert attn_width == 2 * len(HEAD_ORDER[0]) * HEAD_DIM and rel_bias.shape == (N_BUCKETS, attn_width // HEAD_DIM)
    tm_ffn, ff_chunk = _tile_sizes(bsz, seq)
    assert w_down.shape[1] % ff_chunk == 0

    bias = _bias_table(rel_bias)
    ssm_ops = jax.vmap(_ssm_operators)(lam_re, lam_im, log_step, b_re, b_im, c_re, c_im, ssm_d)
    row = lambda a: a.astype(F32).reshape(1, -1)
    q_scale = jnp.concatenate([jnp.ones((ssm_width,), F32), jnp.full((attn_width,), HEAD_DIM ** -0.5, F32),
                               jnp.ones((w_in.shape[2] - ssm_width - attn_width,), F32)])

    x2d = x.transpose(1, 0, 2).reshape(seq * bsz, d_model)
    for l in range(depth):
        u_t, qkv = _proj(x2d, row(pre_mix_norm[l]), (w_in[l] * q_scale).astype(BF16), bsz, ssm_width)
        qkv3 = qkv.reshape(seq, bsz, -1).transpose(1, 0, 2)
        att = _attention(qkv3, bias, attn_sink[l].astype(F32), attn_width)
        att2d = att.transpose(1, 0, 2).reshape(seq * bsz, attn_width)

        z_t = _ssm(u_t.reshape(g_cnt, nc * bsz, -1), *(op[l] for op in ssm_ops), bsz).reshape(u_t.shape)

        x2d = _merge(z_t, att2d, x2d, w_glu[l].astype(BF16), row(b_glu[l]), row(ssm_out_norm[l]),
                     row(attn_out_norm[l]), w_out[l].astype(BF16), row(post_mix_norm[l]), bsz)
        x2d = _ffn(x2d, row(pre_ffn_norm[l]), w_up[l].astype(BF16), conv_w[l].astype(F32), row(conv_b[l]),
                   w_down[l].astype(BF16), row(post_ffn_norm[l]), bsz, tm_ffn, ff_chunk)
    return x2d.reshape(seq, bsz, d_model).transpose(1, 0, 2)
```

```python
import functools
import math

import numpy as np
import jax
import jax.numpy as jnp
from jax import lax
from jax.experimental import pallas as pl
from jax.experimental.pallas import tpu as pltpu

F32 = jnp.float32
BF16 = jnp.bfloat16

LANES = 128
SSM_GROUP = 16
SSM_STATE = 64
HEAD_DIM = 64
N_KV_HEADS = 2
WINDOW = 128
BLOCK = 128
N_BUCKETS = 32
MAX_DISTANCE = 128
EPS = 1e-6
NEG_INF = -1e30
SSM_CHUNK = 16
Q_BLOCKS = 1
GROUPS_PER_VREG = LANES // SSM_GROUP

VMEM_LIMIT_BYTES = 56 * 1024 * 1024

HEAD_ORDER = ((0, 2, 5, 7), (1, 3, 4, 6))


def _params(*sem):
    return pltpu.CompilerParams(dimension_semantics=sem, vmem_limit_bytes=VMEM_LIMIT_BYTES)


def _rms(x, g):
    return x * lax.rsqrt(jnp.mean(x * x, axis=-1, keepdims=True) + EPS) * g


def _gelu(x):
    c = math.sqrt(2.0 / math.pi)
    return x * (0.5 + 0.5 * jnp.tanh(x * (c + (c * 0.044715) * (x * x))))


def _atom_transpose(vs):
    atom = lax.broadcasted_iota(jnp.int32, vs[0].shape, 1) // SSM_GROUP
    for d in (4, 2, 1):
        keep = (atom & d) == 0
        nxt = list(vs)
        for i in range(GROUPS_PER_VREG):
            if i & d == 0:
                a, b = vs[i], vs[i + d]
                nxt[i] = jnp.where(keep, a, pltpu.roll(b, SSM_GROUP * d, axis=1))
                nxt[i + d] = jnp.where(keep, pltpu.roll(a, LANES - SSM_GROUP * d, axis=1), b)
        vs = nxt
    return vs


def _bucket_table():
    half = N_BUCKETS // 2
    max_exact = half // 2
    qi = np.arange(BLOCK)[:, None]
    sj = np.arange(3 * BLOCK)[None, :]
    rel = sj - BLOCK - qi
    n = np.abs(rel)
    nf = np.maximum(n, 1).astype(np.float64)
    large = max_exact + (np.log(nf / max_exact) / math.log(MAX_DISTANCE / max_exact)
                         * (half - max_exact)).astype(np.int32)
    large = np.minimum(large, half - 1)
    bucket = np.where(rel > 0, half, 0) + np.where(n < max_exact, n, large)
    return np.where(n <= WINDOW, bucket, -1).astype(np.int32)


def _bias_kernel(bucket_ref, rel_bias_ref, out_ref):
    bucket = bucket_ref[...]
    col = lax.broadcasted_iota(jnp.int32, bucket.shape, 1)
    for grp, heads in enumerate(HEAD_ORDER):
        for slot, h in enumerate(heads):
            acc = jnp.full(bucket.shape, NEG_INF, F32)
            for b in range(N_BUCKETS):
                acc = jnp.where(bucket == b, rel_bias_ref[b, h], acc)
            for variant in range(4):
                lo = BLOCK if variant & 1 else 0
                hi = 2 * BLOCK if variant & 2 else 3 * BLOCK
                out_ref[variant, grp, slot * BLOCK:(slot + 1) * BLOCK, :] = jnp.where(
                    (col >= lo) & (col < hi), acc, NEG_INF)


def _bias_table(rel_bias):
    rows = len(HEAD_ORDER[0]) * BLOCK
    bucket = jnp.asarray(_bucket_table())
    return pl.pallas_call(
        _bias_kernel,
        out_shape=jax.ShapeDtypeStruct((4, len(HEAD_ORDER), rows, 3 * BLOCK), F32),
        in_specs=[pl.BlockSpec(memory_space=pltpu.VMEM), pl.BlockSpec(memory_space=pltpu.SMEM)],
        out_specs=pl.BlockSpec(memory_space=pltpu.VMEM),
        name="bias_table",
    )(bucket, rel_bias.astype(F32))


def _stream_spec(tm, d, bsz, batch_major):
    if batch_major:
        return pl.BlockSpec((bsz, (tm // bsz) * d), lambda i: (0, i))
    return pl.BlockSpec((tm, d), lambda i: (i, 0))


def _load_stream(x_ref, d, batch_major):
    if not batch_major:
        return x_ref[...]
    return jnp.concatenate([x_ref[:, t * d:(t + 1) * d] for t in range(x_ref.shape[1] // d)], axis=0)


def _proj_kernel(x_ref, g_ref, w_ref, ut_ref, qkv_ref, *, bsz, ssm_width, batch_major):
    h = _rms(_load_stream(x_ref, g_ref.shape[1], batch_major), g_ref[...]).astype(BF16)
    r = jnp.dot(h, w_ref[...], preferred_element_type=F32)
    qkv_ref[...] = r[:, ssm_width:].astype(BF16)
    for t1 in range(SSM_CHUNK // GROUPS_PER_VREG):
        for g1 in range(ssm_width // LANES):
            vs = [r[(t1 * GROUPS_PER_VREG + t2) * bsz:(t1 * GROUPS_PER_VREG + t2 + 1) * bsz,
                    g1 * LANES:(g1 + 1) * LANES] for t2 in range(GROUPS_PER_VREG)]
            for g2, tile in enumerate(_atom_transpose(vs)):
                ut_ref[g1 * GROUPS_PER_VREG + g2, :, t1 * LANES:(t1 + 1) * LANES] = tile.astype(BF16)


def _proj(x2d, g, w, bsz, ssm_width, batch_major):
    d, n = w.shape
    t = x2d.size // d
    tm = SSM_CHUNK * bsz
    nc = t // tm
    g_cnt = ssm_width // SSM_GROUP
    lanes = SSM_CHUNK * SSM_GROUP
    return pl.pallas_call(
        functools.partial(_proj_kernel, bsz=bsz, ssm_width=ssm_width, batch_major=batch_major),
        out_shape=(jax.ShapeDtypeStruct((g_cnt, nc, bsz, lanes), BF16),
                   jax.ShapeDtypeStruct((t, n - ssm_width), BF16)),
        grid=(nc,),
        in_specs=[_stream_spec(tm, d, bsz, batch_major),
                  pl.BlockSpec((1, d), lambda i: (0, 0)),
                  pl.BlockSpec((d, n), lambda i: (0, 0))],
        out_specs=(pl.BlockSpec((g_cnt, None, bsz, lanes), lambda i: (0, i, 0, 0)),
                   pl.BlockSpec((tm, n - ssm_width), lambda i: (i, 0))),
        compiler_params=_params("parallel"),
        name="proj",
    )(x2d, g, w)


def _attn_kernel(qlo_ref, qhi_ref, *refs):
    n_kb = Q_BLOCKS + 2
    k_refs, v_refs = refs[:n_kb], refs[n_kb:2 * n_kb]
    bias_ref, sink_ref, o_ref = refs[2 * n_kb:]
    j = pl.program_id(1)
    nj = pl.num_programs(1)
    k = jnp.concatenate([r[...] for r in k_refs], axis=0)
    v = jnp.concatenate([r[...] for r in v_refs], axis=0)
    swap = lambda a: jnp.concatenate([a[:, HEAD_DIM:], a[:, :HEAD_DIM]], axis=1)
    k_sw, v_sw = swap(k), swap(v)
    low = lax.broadcasted_iota(jnp.int32, k.shape, 1) < HEAD_DIM
    zero = jnp.zeros_like(k)
    k_even = (jnp.where(low, k, zero), jnp.where(low, k_sw, zero))
    k_odd = (jnp.where(low, zero, k_sw), jnp.where(low, zero, k))
    ones = jnp.ones_like(v)
    v_ext = (jnp.concatenate([v, ones], axis=1), jnp.concatenate([v_sw, ones], axis=1))
    qk = lambda a, b: lax.dot_general(a, b, (((1,), (1,)), ((), ())), preferred_element_type=F32)
    low_o = lax.broadcasted_iota(jnp.int32, (BLOCK, LANES), 1) < HEAD_DIM
    for sub in range(Q_BLOCKS):
        variant = jnp.where(j == 0, 1, 0) if sub == 0 else 0
        variant = variant + (jnp.where(j == nj - 1, 2, 0) if sub == Q_BLOCKS - 1 else 0)
        qrows = slice(sub * BLOCK, (sub + 1) * BLOCK)
        keys = lambda a: a[sub * BLOCK:(sub + 3) * BLOCK]
        q_kv = (jnp.concatenate([qlo_ref[qrows, 0:LANES], qlo_ref[qrows, LANES:2 * LANES]], axis=0),
                jnp.concatenate([qhi_ref[qrows, 0:LANES], qhi_ref[qrows, LANES:2 * LANES]], axis=0))
        scores = (jnp.concatenate([qk(q_kv[0], keys(k_even[0])), qk(q_kv[1], keys(k_odd[1]))], axis=0),
                  jnp.concatenate([qk(q_kv[0], keys(k_odd[0])), qk(q_kv[1], keys(k_even[1]))], axis=0))
        outs = []
        for grp, heads in enumerate(HEAD_ORDER):
            sink = jnp.concatenate([jnp.full((BLOCK, 1), sink_ref[h], F32) for h in heads], axis=0)
            s = scores[grp] + bias_ref[variant, grp]
            m = jnp.maximum(jnp.max(s, axis=-1, keepdims=True), sink)
            e = jnp.exp(s - m).astype(BF16)
            o = jnp.dot(e, keys(v_ext[grp]), preferred_element_type=F32)
            outs.append(o[:, :LANES] / (o[:, LANES:] + jnp.exp(sink - m)))
        o_a, o_b = outs
        rows = lambda a, slot: a[slot * BLOCK:(slot + 1) * BLOCK]
        pairs = [jnp.where(low_o, rows(o_a, 0), rows(o_b, 0)), jnp.where(low_o, rows(o_a, 1), rows(o_b, 1)),
                 jnp.where(low_o, rows(o_b, 2), rows(o_a, 2)), jnp.where(low_o, rows(o_b, 3), rows(o_a, 3))]
        o_ref[qrows, :] = jnp.concatenate(pairs, axis=1).astype(BF16)


def _attention(qkv2d, bias, sink, bsz, attn_width):
    seq = qkv2d.shape[0]
    kv_width = N_KV_HEADS * HEAD_DIM
    width = attn_width + 2 * kv_width
    q_half = attn_width // 2
    assert qkv2d.shape[1] == bsz * width and width % q_half == 0 and seq % (Q_BLOCKS * BLOCK) == 0
    nb = seq // BLOCK
    rows = Q_BLOCKS * BLOCK
    q_spec = lambda half: pl.BlockSpec((rows, q_half), lambda b, j: (j, b * (width // q_half) + half))
    kv_spec = lambda col, off: pl.BlockSpec(
        (BLOCK, kv_width),
        lambda b, j: (jnp.clip(j * Q_BLOCKS + off, 0, nb - 1), b * (width // kv_width) + col))
    k_col = attn_width // kv_width
    offsets = range(-1, Q_BLOCKS + 1)
    return pl.pallas_call(
        _attn_kernel,
        out_shape=jax.ShapeDtypeStruct((seq, bsz * attn_width), BF16),
        grid=(bsz, nb // Q_BLOCKS),
        in_specs=[q_spec(0), q_spec(1)]
                 + [kv_spec(k_col, off) for off in offsets] + [kv_spec(k_col + 1, off) for off in offsets]
                 + [pl.BlockSpec(bias.shape, lambda b, j: (0, 0, 0, 0)), pl.BlockSpec(memory_space=pltpu.SMEM)],
        out_specs=pl.BlockSpec((rows, attn_width), lambda b, j: (j, b)),
        compiler_params=_params("parallel", "parallel"),
        name="attention",
    )(*([qkv2d] * (2 + 2 * len(offsets))), bias, sink)


def _ssm_operators(lam_re, lam_im, log_step, b_re, b_im, c_re, c_im, d):
    hp = lax.Precision.HIGHEST
    t_len = SSM_CHUNK
    lam_re, lam_im = lam_re.astype(F32), lam_im.astype(F32)
    dt = jnp.exp(log_step.astype(F32))[..., None]
    ld_re, ld_im = lam_re * dt, lam_im * dt
    mag = jnp.exp(ld_re)
    lb_re, lb_im = mag * jnp.cos(ld_im), mag * jnp.sin(ld_im)
    den = lam_re * lam_re + lam_im * lam_im
    nr, ni = lb_re - 1.0, lb_im
    coef_re = (nr * lam_re + ni * lam_im) / den
    coef_im = (ni * lam_re - nr * lam_im) / den
    b_re, b_im = b_re.astype(F32), b_im.astype(F32)
    bb_re = coef_re[..., None] * b_re - coef_im[..., None] * b_im
    bb_im = coef_re[..., None] * b_im + coef_im[..., None] * b_re
    g_cnt, n_cnt, p_cnt = bb_re.shape[1:]

    def powers(re, im, tau):
        m = jnp.exp(re * tau)
        return m * jnp.cos(im * tau), m * jnp.sin(im * tau)

    tau = jnp.arange(t_len + 1, dtype=F32)
    pw_re, pw_im = powers(ld_re[..., None], ld_im[..., None], tau)
    pwt_re, pwt_im = powers(ld_re[:, :, None], ld_im[:, :, None], tau[:, None])

    ct_re = c_re.astype(F32).transpose(0, 1, 3, 2)[:, :, :, None, :]
    ct_im = c_im.astype(F32).transpose(0, 1, 3, 2)[:, :, :, None, :]
    cp_re = ct_re * pw_re[..., None] - ct_im * pw_im[..., None]
    cp_im = ct_re * pw_im[..., None] + ct_im * pw_re[..., None]
    k_lag = (jnp.einsum('dgnq,dgntp->dgqtp', bb_re, cp_re[:, :, :, :t_len], precision=hp)
             - jnp.einsum('dgnq,dgntp->dgqtp', bb_im, cp_im[:, :, :, :t_len], precision=hp))
    skip = d.astype(F32).reshape(g_cnt, 1, p_cnt) * jnp.eye(p_cnt, dtype=F32)
    k_fwd = k_lag[0].at[:, :, 0].add(skip)
    zeros = jnp.zeros_like(k_fwd)
    f_pad = jnp.concatenate([zeros, k_fwd], axis=2)
    b_pad = jnp.concatenate([k_lag[1][:, :, ::-1], zeros], axis=2)
    kbig = jnp.stack([f_pad[:, :, t_len - s:2 * t_len - s] + b_pad[:, :, t_len - 1 - s:2 * t_len - 1 - s]
                      for s in range(t_len)], axis=1)
    kbig = kbig.reshape(g_cnt, t_len * p_cnt, t_len * p_cnt)

    def lift_b(direction, pr, pi):
        br = bb_re[direction].transpose(0, 2, 1)[:, None]
        bi = bb_im[direction].transpose(0, 2, 1)[:, None]
        pr, pi = pr[:, :, None], pi[:, :, None]
        return pr * br - pi * bi, pr * bi + pi * br

    bf_re, bf_im = lift_b(0, pwt_re[0][:, :t_len][:, ::-1], pwt_im[0][:, :t_len][:, ::-1])
    bw_re, bw_im = lift_b(1, pwt_re[1][:, :t_len], pwt_im[1][:, :t_len])
    bbig = jnp.concatenate([bf_re, bw_re, bf_im, bw_im], axis=-1).reshape(g_cnt, t_len * p_cnt, 4 * n_cnt)

    fwd = lambda m: m[0][:, :, 1:t_len + 1]
    bwd = lambda m: m[1][:, :, 1:t_len + 1][:, :, ::-1]
    cbig = jnp.concatenate([fwd(cp_re), bwd(cp_re), -fwd(cp_im), -bwd(cp_im)], axis=1)
    cbig = cbig.reshape(g_cnt, 4 * n_cnt, t_len * p_cnt)

    a_re = jnp.concatenate([pw_re[0][..., t_len], pw_re[1][..., t_len]], axis=-1)
    a_im = jnp.concatenate([pw_im[0][..., t_len], pw_im[1][..., t_len]], axis=-1)
    a16 = jnp.stack([a_re, a_im], axis=1)
    return bbig.astype(BF16), kbig.astype(BF16), cbig.astype(BF16), a16


def _ssm_kernel(u_ref, bbig_ref, kbig_ref, cbig_ref, a_ref, z_ref, v_scr, s_scr, sp_scr, *, bsz, rows_per_dot):
    rows = u_ref.shape[1]
    nc = rows // bsz
    half = 2 * SSM_STATE

    def local_states(i, carry):
        r = pl.multiple_of(i * rows_per_dot, rows_per_dot)
        v_scr[pl.ds(r, rows_per_dot), :] = jnp.dot(
            u_ref[0, pl.ds(r, rows_per_dot), :], bbig_ref[0], preferred_element_type=F32)
        return carry

    lax.fori_loop(0, rows // rows_per_dot, local_states, 0)

    a_re = a_ref[0, 0:1, :]
    a_im = a_ref[0, 1:2, :]
    is_fwd = lax.broadcasted_iota(jnp.int32, (bsz, half), 1) < SSM_STATE

    def scan_step(i, carry):
        s_re, s_im = carry
        rf = pl.multiple_of(i * bsz, bsz)
        rb = pl.multiple_of((nc - 1 - i) * bsz, bsz)
        s_scr[pl.ds(rf, bsz), 0:half] = s_re
        s_scr[pl.ds(rf, bsz), half:2 * half] = s_im
        v_re = jnp.where(is_fwd, v_scr[pl.ds(rf, bsz), 0:half], v_scr[pl.ds(rb, bsz), 0:half])
        v_im = jnp.where(is_fwd, v_scr[pl.ds(rf, bsz), half:2 * half], v_scr[pl.ds(rb, bsz), half:2 * half])
        n_re = a_re * s_re - a_im * s_im + v_re
        n_im = a_re * s_im + a_im * s_re + v_im
        return n_re, n_im

    zero = jnp.zeros((bsz, half), F32)
    lax.fori_loop(0, nc, scan_step, (zero, zero))

    is_fwd2 = lax.broadcasted_iota(jnp.int32, (bsz, 2 * half), 1) % half < SSM_STATE

    def entering_states(c, carry):
        rf = pl.multiple_of(c * bsz, bsz)
        rb = pl.multiple_of((nc - 1 - c) * bsz, bsz)
        sp_scr[pl.ds(rf, bsz), :] = jnp.where(
            is_fwd2, s_scr[pl.ds(rf, bsz), :], s_scr[pl.ds(rb, bsz), :]).astype(BF16)
        return carry

    lax.fori_loop(0, nc, entering_states, 0)

    def outputs(i, carry):
        r = pl.multiple_of(i * rows_per_dot, rows_per_dot)
        y = (jnp.dot(u_ref[0, pl.ds(r, rows_per_dot), :], kbig_ref[0], preferred_element_type=F32)
             + jnp.dot(sp_scr[pl.ds(r, rows_per_dot), :], cbig_ref[0], preferred_element_type=F32))
        z_ref[0, pl.ds(r, rows_per_dot), :] = _gelu(y).astype(BF16)
        return carry

    lax.fori_loop(0, rows // rows_per_dot, outputs, 0)


def _ssm(u_t, bbig, kbig, cbig, a16, bsz):
    g_cnt, rows, width = u_t.shape
    rows_per_dot = min(rows, 512)
    n_state = bbig.shape[-1]
    return pl.pallas_call(
        functools.partial(_ssm_kernel, bsz=bsz, rows_per_dot=rows_per_dot),
        out_shape=jax.ShapeDtypeStruct((g_cnt, rows, width), BF16),
        grid=(g_cnt,),
        in_specs=[pl.BlockSpec((1, rows, width), lambda g: (g, 0, 0)),
                  pl.BlockSpec((1, width, n_state), lambda g: (g, 0, 0)),
                  pl.BlockSpec((1, width, width), lambda g: (g, 0, 0)),
                  pl.BlockSpec((1, n_state, width), lambda g: (g, 0, 0)),
                  pl.BlockSpec((1, 2, n_state // 2), lambda g: (g, 0, 0))],
        out_specs=pl.BlockSpec((1, rows, width), lambda g: (g, 0, 0)),
        scratch_shapes=[pltpu.VMEM((rows, n_state), F32),
                        pltpu.VMEM((rows, n_state), F32),
                        pltpu.VMEM((rows, n_state), BF16)],
        compiler_params=_params("parallel"),
        name="s5_mixer",
    )(u_t, bbig, kbig, cbig, a16)


def _merge_kernel(zt_ref, att_ref, x_ref, wglu_ref, bglu_ref, gssm_ref, gatt_ref, wout_ref, gpost_ref, o_ref,
                  z_scr, *, bsz, batch_major):
    for t1 in range(SSM_CHUNK // GROUPS_PER_VREG):
        for g1 in range(z_scr.shape[1] // LANES):
            vs = [zt_ref[g1 * GROUPS_PER_VREG + g2, :, t1 * LANES:(t1 + 1) * LANES].astype(F32)
                  for g2 in range(GROUPS_PER_VREG)]
            for t2, tile in enumerate(_atom_transpose(vs)):
                t = t1 * GROUPS_PER_VREG + t2
                z_scr[t * bsz:(t + 1) * bsz, g1 * LANES:(g1 + 1) * LANES] = tile
    z = z_scr[...]
    gate = jnp.dot(z.astype(BF16), wglu_ref[...], preferred_element_type=F32) + bglu_ref[...]
    y_ssm = z * jax.nn.sigmoid(gate)
    y_att = att_ref[...].astype(F32)
    merged = jnp.concatenate([_rms(y_ssm, gssm_ref[...]).astype(BF16),
                              _rms(y_att, gatt_ref[...]).astype(BF16)], axis=-1)
    y = jnp.dot(merged, wout_ref[...], preferred_element_type=F32)
    o_ref[...] = _load_stream(x_ref, o_ref.shape[1], batch_major) + _rms(y, gpost_ref[...])


def _merge(z_t, att2d, x2d, w_glu, b_glu, g_ssm, g_att, w_out, g_post, bsz, batch_major):
    d = w_out.shape[1]
    t = x2d.size // d
    g_cnt, nc, _, lanes = z_t.shape
    ws, wa = g_cnt * SSM_GROUP, att2d.shape[1]
    tm = SSM_CHUNK * bsz
    row = lambda w: pl.BlockSpec((tm, w), lambda i: (i, 0))
    full = lambda a: pl.BlockSpec(a.shape, lambda i: (0,) * a.ndim)
    return pl.pallas_call(
        functools.partial(_merge_kernel, bsz=bsz, batch_major=batch_major),
        out_shape=jax.ShapeDtypeStruct((t, d), F32),
        grid=(nc,),
        in_specs=[pl.BlockSpec((g_cnt, None, bsz, lanes), lambda i: (0, i, 0, 0)),
                  row(wa), _stream_spec(tm, d, bsz, batch_major), full(w_glu), full(b_glu), full(g_ssm), full(g_att),
                  full(w_out), full(g_post)],
        out_specs=row(d),
        scratch_shapes=[pltpu.VMEM((tm, ws), F32)],
        compiler_params=_params("parallel"),
        name="merge",
    )(z_t, att2d, x2d, w_glu, b_glu, g_ssm, g_att, w_out, g_post)


def _ffn_kernel(xp_ref, x_ref, xn_ref, gpre_ref, wup_ref, cw_ref, cb_ref, wdown_ref, gpost_ref, o_ref,
                h_scr, a_scr, act_scr, *, d_ff, ff_chunk, halo, out_batch_major):
    i = pl.program_id(0)
    n_tiles = pl.num_programs(0)
    tm = x_ref.shape[0]
    g = gpre_ref[...]
    keep_prev = jnp.where(i == 0, 0.0, 1.0)
    keep_next = jnp.where(i == n_tiles - 1, 0.0, 1.0)
    h_scr[0:halo, :] = (_rms(xp_ref[...], g) * keep_prev).astype(BF16)
    h_scr[halo:halo + tm, :] = _rms(x_ref[...], g).astype(BF16)
    h_scr[halo + tm:, :] = (_rms(xn_ref[...], g) * keep_next).astype(BF16)
    n_chunks = d_ff // ff_chunk

    def cols(base, c):
        return pl.ds(base + c * ff_chunk, ff_chunk)

    def up(c, slot):
        h = h_scr[...]
        for part, base in enumerate((0, d_ff)):
            a_scr[slot, part] = jnp.dot(h, wup_ref[:, cols(base, c)], preferred_element_type=F32)

    def post(c, slot):
        def conv(part, base):
            w = cw_ref[:, cols(base, c)]
            return (a_scr[slot, part, 0:tm] * w[0:1] + a_scr[slot, part, halo:halo + tm] * w[1:2]
                    + a_scr[slot, part, 2 * halo:2 * halo + tm] * w[2:3] + cb_ref[:, cols(base, c)])

        act = _gelu(conv(1, d_ff)) * conv(0, 0)
        act_scr[:, cols(0, c)] = act.astype(BF16)

    n_slots = a_scr.shape[0]
    up(0, 0)
    for c in range(n_chunks - 1):
        up(c + 1, (c + 1) % n_slots)
        post(c, c % n_slots)
    k_head = (n_chunks - 1) * ff_chunk
    y = jnp.dot(act_scr[:, :k_head], wdown_ref[:k_head, :], preferred_element_type=F32)
    post(n_chunks - 1, (n_chunks - 1) % n_slots)
    y = y + jnp.dot(act_scr[:, k_head:], wdown_ref[k_head:, :], preferred_element_type=F32)
    out = x_ref[...] + _rms(y, gpost_ref[...])
    if out_batch_major:
        d = out.shape[1]
        for t in range(tm // halo):
            o_ref[:, t * d:(t + 1) * d] = out[t * halo:(t + 1) * halo]
    else:
        o_ref[...] = out


def _ffn(x2d, g_pre, w_up, conv_w, conv_b, w_down, g_post, bsz, tm, ff_chunk, out_batch_major):
    t, d = x2d.shape
    d_ff = w_down.shape[0]
    halo = bsz
    hb = tm // halo
    n_halo_blocks = t // halo
    full = lambda a: pl.BlockSpec(a.shape, lambda i: (0,) * a.ndim)
    return pl.pallas_call(
        functools.partial(_ffn_kernel, d_ff=d_ff, ff_chunk=ff_chunk, halo=halo, out_batch_major=out_batch_major),
        out_shape=jax.ShapeDtypeStruct((bsz, (t // bsz) * d) if out_batch_major else (t, d), F32),
        grid=(t // tm,),
        in_specs=[pl.BlockSpec((halo, d), lambda i: (jnp.maximum(i * hb - 1, 0), 0)),
                  pl.BlockSpec((tm, d), lambda i: (i, 0)),
                  pl.BlockSpec((halo, d), lambda i: (jnp.minimum((i + 1) * hb, n_halo_blocks - 1), 0)),
                  full(g_pre), full(w_up), full(conv_w), full(conv_b), full(w_down), full(g_post)],
        out_specs=_stream_spec(tm, d, bsz, out_batch_major),
        scratch_shapes=[pltpu.VMEM((tm + 2 * halo, d), BF16),
                        pltpu.VMEM((3, 2, tm + 2 * halo, ff_chunk), F32),
                        pltpu.VMEM((tm, d_ff), BF16)],
        compiler_params=_params("parallel"),
        name="conv_ffn",
    )(x2d, x2d, x2d, g_pre, w_up, conv_w, conv_b, w_down, g_post)


def _tile_sizes(bsz, seq):
    steps = 32 if seq % 32 == 0 else 1
    return steps * bsz, 256


def kernel(x, rel_bias, pre_mix_norm, w_in, lam_re, lam_im, log_step, b_re, b_im, c_re, c_im, ssm_d, w_glu,
           b_glu, attn_sink, ssm_out_norm, attn_out_norm, w_out, post_mix_norm, pre_ffn_norm, w_up, conv_w,
           conv_b, w_down, post_ffn_norm):
    bsz, seq, d_model = x.shape
    depth = w_in.shape[0]
    ssm_width = ssm_d.shape[1]
    attn_width = w_out.shape[1] - ssm_width
    g_cnt = ssm_width // SSM_GROUP
    nc = seq // SSM_CHUNK
    assert seq % BLOCK == 0 and bsz % 16 == 0 and ssm_width % LANES == 0
    assert attn_width == 2 * len(HEAD_ORDER[0]) * HEAD_DIM and rel_bias.shape == (N_BUCKETS, attn_width // HEAD_DIM)
    tm_ffn, ff_chunk = _tile_sizes(bsz, seq)
    assert w_down.shape[1] % ff_chunk == 0

    bias = _bias_table(rel_bias)
    ssm_ops = jax.vmap(_ssm_operators)(lam_re, lam_im, log_step, b_re, b_im, c_re, c_im, ssm_d)
    row = lambda a: a.astype(F32).reshape(1, -1)
    q_scale = jnp.concatenate([jnp.ones((ssm_width,), F32), jnp.full((attn_width,), HEAD_DIM ** -0.5, F32),
                               jnp.ones((w_in.shape[2] - ssm_width - attn_width,), F32)])

    x2d = x.reshape(bsz, seq * d_model)
    for l in range(depth):
        first, last = l == 0, l == depth - 1
        u_t, qkv = _proj(x2d, row(pre_mix_norm[l]), (w_in[l] * q_scale).astype(BF16), bsz, ssm_width, first)
        att = _attention(qkv.reshape(seq, -1), bias, attn_sink[l].astype(F32), bsz, attn_width)
        att2d = att.reshape(seq * bsz, attn_width)

        z_t = _ssm(u_t.reshape(g_cnt, nc * bsz, -1), *(op[l] for op in ssm_ops), bsz).reshape(u_t.shape)

        x2d = _merge(z_t, att2d, x2d, w_glu[l].astype(BF16), row(b_glu[l]), row(ssm_out_norm[l]),
                     row(attn_out_norm[l]), w_out[l].astype(BF16), row(post_mix_norm[l]), bsz, first)
        x2d = _ffn(x2d, row(pre_ffn_norm[l]), w_up[l].astype(BF16), conv_w[l].astype(F32), row(conv_b[l]),
                   w_down[l].astype(BF16), row(post_ffn_norm[l]), bsz, tm_ffn, ff_chunk, last)
    return x2d.reshape(bsz, seq, d_model)
```

```python
import functools
import math

import numpy as np
import jax
import jax.numpy as jnp
from jax import lax
from jax.experimental import pallas as pl
from jax.experimental.pallas import tpu as pltpu

F32 = jnp.float32
BF16 = jnp.bfloat16

LANES = 128
SSM_GROUP = 16
SSM_STATE = 64
HEAD_DIM = 64
N_KV_HEADS = 2
WINDOW = 128
BLOCK = 128
N_BUCKETS = 32
MAX_DISTANCE = 128
EPS = 1e-6
NEG_INF = -1e30
SSM_CHUNK = 16
Q_BLOCKS = 1
GROUPS_PER_VREG = LANES // SSM_GROUP

VMEM_LIMIT_BYTES = 56 * 1024 * 1024

HEAD_ORDER = ((0, 2, 5, 7), (1, 3, 4, 6))


def _params(*sem):
    return pltpu.CompilerParams(dimension_semantics=sem, vmem_limit_bytes=VMEM_LIMIT_BYTES)


def _rms(x, g):
    return x * lax.rsqrt(jnp.mean(x * x, axis=-1, keepdims=True) + EPS) * g


def _gelu(x):
    c = math.sqrt(2.0 / math.pi)
    return x * (0.5 + 0.5 * jnp.tanh(x * (c + (c * 0.044715) * (x * x))))


def _atom_transpose(vs):
    atom = lax.broadcasted_iota(jnp.int32, vs[0].shape, 1) // SSM_GROUP
    for d in (4, 2, 1):
        keep = (atom & d) == 0
        nxt = list(vs)
        for i in range(GROUPS_PER_VREG):
            if i & d == 0:
                a, b = vs[i], vs[i + d]
                nxt[i] = jnp.where(keep, a, pltpu.roll(b, SSM_GROUP * d, axis=1))
                nxt[i + d] = jnp.where(keep, pltpu.roll(a, LANES - SSM_GROUP * d, axis=1), b)
        vs = nxt
    return vs


def _bucket_table():
    half = N_BUCKETS // 2
    max_exact = half // 2
    qi = np.arange(BLOCK)[:, None]
    sj = np.arange(3 * BLOCK)[None, :]
    rel = sj - BLOCK - qi
    n = np.abs(rel)
    nf = np.maximum(n, 1).astype(np.float64)
    large = max_exact + (np.log(nf / max_exact) / math.log(MAX_DISTANCE / max_exact)
                         * (half - max_exact)).astype(np.int32)
    large = np.minimum(large, half - 1)
    bucket = np.where(rel > 0, half, 0) + np.where(n < max_exact, n, large)
    return np.where(n <= WINDOW, bucket, -1).astype(np.int32)


def _bias_kernel(bucket_ref, rel_bias_ref, out_ref):
    bucket = bucket_ref[...]
    col = lax.broadcasted_iota(jnp.int32, bucket.shape, 1)
    for grp, heads in enumerate(HEAD_ORDER):
        for slot, h in enumerate(heads):
            acc = jnp.full(bucket.shape, NEG_INF, F32)
            for b in range(N_BUCKETS):
                acc = jnp.where(bucket == b, rel_bias_ref[b, h], acc)
            for variant in range(4):
                lo = BLOCK if variant & 1 else 0
                hi = 2 * BLOCK if variant & 2 else 3 * BLOCK
                out_ref[variant, grp, slot * BLOCK:(slot + 1) * BLOCK, :] = jnp.where(
                    (col >= lo) & (col < hi), acc, NEG_INF)


def _bias_table(rel_bias):
    rows = len(HEAD_ORDER[0]) * BLOCK
    bucket = jnp.asarray(_bucket_table())
    return pl.pallas_call(
        _bias_kernel,
        out_shape=jax.ShapeDtypeStruct((4, len(HEAD_ORDER), rows, 3 * BLOCK), F32),
        in_specs=[pl.BlockSpec(memory_space=pltpu.VMEM), pl.BlockSpec(memory_space=pltpu.SMEM)],
        out_specs=pl.BlockSpec(memory_space=pltpu.VMEM),
        name="bias_table",
    )(bucket, rel_bias.astype(F32))


def _stream_spec(tm, d, bsz, batch_major):
    if batch_major:
        return pl.BlockSpec((bsz, (tm // bsz) * d), lambda i: (0, i))
    return pl.BlockSpec((tm, d), lambda i: (i, 0))


def _load_stream(x_ref, d, batch_major):
    if not batch_major:
        return x_ref[...]
    return jnp.concatenate([x_ref[:, t * d:(t + 1) * d] for t in range(x_ref.shape[1] // d)], axis=0)


def _proj_kernel(x_ref, g_ref, w_ref, ut_ref, qkv_ref, *, bsz, ssm_width, batch_major):
    h = _rms(_load_stream(x_ref, g_ref.shape[1], batch_major), g_ref[...]).astype(BF16)
    r = jnp.dot(h, w_ref[...], preferred_element_type=F32)
    qkv_ref[...] = r[:, ssm_width:].astype(BF16)
    for t1 in range(SSM_CHUNK // GROUPS_PER_VREG):
        for g1 in range(ssm_width // LANES):
            vs = [r[(t1 * GROUPS_PER_VREG + t2) * bsz:(t1 * GROUPS_PER_VREG + t2 + 1) * bsz,
                    g1 * LANES:(g1 + 1) * LANES] for t2 in range(GROUPS_PER_VREG)]
            for g2, tile in enumerate(_atom_transpose(vs)):
                ut_ref[g1 * GROUPS_PER_VREG + g2, :, t1 * LANES:(t1 + 1) * LANES] = tile.astype(BF16)


def _proj(x2d, g, w, bsz, ssm_width, batch_major):
    d, n = w.shape
    t = x2d.size // d
    tm = SSM_CHUNK * bsz
    nc = t // tm
    g_cnt = ssm_width // SSM_GROUP
    lanes = SSM_CHUNK * SSM_GROUP
    return pl.pallas_call(
        functools.partial(_proj_kernel, bsz=bsz, ssm_width=ssm_width, batch_major=batch_major),
        out_shape=(jax.ShapeDtypeStruct((g_cnt, nc, bsz, lanes), BF16),
                   jax.ShapeDtypeStruct((t, n - ssm_width), BF16)),
        grid=(nc,),
        in_specs=[_stream_spec(tm, d, bsz, batch_major),
                  pl.BlockSpec((1, d), lambda i: (0, 0)),
                  pl.BlockSpec((d, n), lambda i: (0, 0))],
        out_specs=(pl.BlockSpec((g_cnt, None, bsz, lanes), lambda i: (0, i, 0, 0)),
                   pl.BlockSpec((tm, n - ssm_width), lambda i: (i, 0))),
        compiler_params=_params("parallel"),
        name="proj",
    )(x2d, g, w)


def _attn_kernel(qlo_ref, qhi_ref, *refs):
    n_kb = Q_BLOCKS + 2
    k_refs, v_refs = refs[:n_kb], refs[n_kb:2 * n_kb]
    bias_ref, sink_ref, o_ref = refs[2 * n_kb:]
    j = pl.program_id(1)
    nj = pl.num_programs(1)
    k = jnp.concatenate([r[...] for r in k_refs], axis=0)
    v = jnp.concatenate([r[...] for r in v_refs], axis=0)
    swap = lambda a: jnp.concatenate([a[:, HEAD_DIM:], a[:, :HEAD_DIM]], axis=1)
    k_sw, v_sw = swap(k), swap(v)
    low = lax.broadcasted_iota(jnp.int32, k.shape, 1) < HEAD_DIM
    zero = jnp.zeros_like(k)
    k_even = (jnp.where(low, k, zero), jnp.where(low, k_sw, zero))
    k_odd = (jnp.where(low, zero, k_sw), jnp.where(low, zero, k))
    ones = jnp.ones_like(v)
    v_ext = (jnp.concatenate([v, ones], axis=1), jnp.concatenate([v_sw, ones], axis=1))
    qk = lambda a, b: lax.dot_general(a, b, (((1,), (1,)), ((), ())), preferred_element_type=F32)
    low_o = lax.broadcasted_iota(jnp.int32, (BLOCK, LANES), 1) < HEAD_DIM
    for sub in range(Q_BLOCKS):
        variant = jnp.where(j == 0, 1, 0) if sub == 0 else 0
        variant = variant + (jnp.where(j == nj - 1, 2, 0) if sub == Q_BLOCKS - 1 else 0)
        qrows = slice(sub * BLOCK, (sub + 1) * BLOCK)
        keys = lambda a: a[sub * BLOCK:(sub + 3) * BLOCK]
        q_kv = (jnp.concatenate([qlo_ref[qrows, 0:LANES], qlo_ref[qrows, LANES:2 * LANES]], axis=0),
                jnp.concatenate([qhi_ref[qrows, 0:LANES], qhi_ref[qrows, LANES:2 * LANES]], axis=0))
        scores = (jnp.concatenate([qk(q_kv[0], keys(k_even[0])), qk(q_kv[1], keys(k_odd[1]))], axis=0),
                  jnp.concatenate([qk(q_kv[0], keys(k_odd[0])), qk(q_kv[1], keys(k_even[1]))], axis=0))
        outs = []
        for grp, heads in enumerate(HEAD_ORDER):
            sink = jnp.concatenate([jnp.full((BLOCK, LANES), sink_ref[h], F32) for h in heads], axis=0)
            s = scores[grp] + bias_ref[variant, grp]
            m = jnp.maximum(jnp.broadcast_to(jnp.max(s, axis=-1, keepdims=True), sink.shape), sink)
            e = jnp.concatenate([jnp.exp(s[:, c * LANES:(c + 1) * LANES] - m)
                                 for c in range(s.shape[1] // LANES)], axis=1).astype(BF16)
            o = jnp.dot(e, keys(v_ext[grp]), preferred_element_type=F32)
            outs.append(o[:, :LANES] / (o[:, LANES:] + jnp.exp(sink - m)))
        o_a, o_b = outs
        rows = lambda a, slot: a[slot * BLOCK:(slot + 1) * BLOCK]
        pairs = [jnp.where(low_o, rows(o_a, 0), rows(o_b, 0)), jnp.where(low_o, rows(o_a, 1), rows(o_b, 1)),
                 jnp.where(low_o, rows(o_b, 2), rows(o_a, 2)), jnp.where(low_o, rows(o_b, 3), rows(o_a, 3))]
        o_ref[qrows, :] = jnp.concatenate(pairs, axis=1).astype(BF16)


def _attention(qkv3, bias, sink, attn_width):
    bsz, seq, width = qkv3.shape
    kv_width = N_KV_HEADS * HEAD_DIM
    q_half = attn_width // 2
    assert width == attn_width + 2 * kv_width and seq % (Q_BLOCKS * BLOCK) == 0
    nb = seq // BLOCK
    rows = Q_BLOCKS * BLOCK
    q_spec = lambda half: pl.BlockSpec((None, rows, q_half), lambda b, j: (b, j, half))
    kv_spec = lambda col, off: pl.BlockSpec(
        (None, BLOCK, kv_width), lambda b, j: (b, jnp.clip(j * Q_BLOCKS + off, 0, nb - 1), col))
    k_col = attn_width // kv_width
    offsets = range(-1, Q_BLOCKS + 1)
    return pl.pallas_call(
        _attn_kernel,
        out_shape=jax.ShapeDtypeStruct((bsz, seq, attn_width), BF16),
        grid=(bsz, nb // Q_BLOCKS),
        in_specs=[q_spec(0), q_spec(1)]
                 + [kv_spec(k_col, off) for off in offsets] + [kv_spec(k_col + 1, off) for off in offsets]
                 + [pl.BlockSpec(bias.shape, lambda b, j: (0, 0, 0, 0)), pl.BlockSpec(memory_space=pltpu.SMEM)],
        out_specs=pl.BlockSpec((None, rows, attn_width), lambda b, j: (b, j, 0)),
        compiler_params=_params("parallel", "parallel"),
        name="attention",
    )(*([qkv3] * (2 + 2 * len(offsets))), bias, sink)


def _ssm_operators(lam_re, lam_im, log_step, b_re, b_im, c_re, c_im, d):
    hp = lax.Precision.HIGH
    t_len = SSM_CHUNK
    lam_re, lam_im = lam_re.astype(F32), lam_im.astype(F32)
    dt = jnp.exp(log_step.astype(F32))[..., None]
    ld_re, ld_im = lam_re * dt, lam_im * dt
    mag = jnp.exp(ld_re)
    lb_re, lb_im = mag * jnp.cos(ld_im), mag * jnp.sin(ld_im)
    den = lam_re * lam_re + lam_im * lam_im
    nr, ni = lb_re - 1.0, lb_im
    coef_re = (nr * lam_re + ni * lam_im) / den
    coef_im = (ni * lam_re - nr * lam_im) / den
    b_re, b_im = b_re.astype(F32), b_im.astype(F32)
    bb_re = coef_re[..., None] * b_re - coef_im[..., None] * b_im
    bb_im = coef_re[..., None] * b_im + coef_im[..., None] * b_re
    g_cnt, n_cnt, p_cnt = bb_re.shape[1:]

    def powers(re, im, tau):
        m = jnp.exp(re * tau)
        return m * jnp.cos(im * tau), m * jnp.sin(im * tau)

    tau = jnp.arange(t_len + 1, dtype=F32)
    pw_re, pw_im = powers(ld_re[..., None], ld_im[..., None], tau)
    pwt_re, pwt_im = powers(ld_re[:, :, None], ld_im[:, :, None], tau[:, None])

    ct_re = c_re.astype(F32).transpose(0, 1, 3, 2)[:, :, :, None, :]
    ct_im = c_im.astype(F32).transpose(0, 1, 3, 2)[:, :, :, None, :]
    cp_re = ct_re * pw_re[..., None] - ct_im * pw_im[..., None]
    cp_im = ct_re * pw_im[..., None] + ct_im * pw_re[..., None]
    k_lag = jnp.einsum('dgnq,dgntp->dgqtp', jnp.concatenate([bb_re, -bb_im], axis=2),
                       jnp.concatenate([cp_re, cp_im], axis=2)[:, :, :, :t_len], precision=hp)
    skip = d.astype(F32).reshape(g_cnt, 1, p_cnt) * jnp.eye(p_cnt, dtype=F32)
    k_fwd = k_lag[0].at[:, :, 0].add(skip)
    zeros = jnp.zeros_like(k_fwd)
    f_pad = jnp.concatenate([zeros, k_fwd], axis=2)
    b_pad = jnp.concatenate([k_lag[1][:, :, ::-1], zeros], axis=2)
    kbig = jnp.stack([f_pad[:, :, t_len - s:2 * t_len - s] + b_pad[:, :, t_len - 1 - s:2 * t_len - 1 - s]
                      for s in range(t_len)], axis=1)
    kbig = kbig.reshape(g_cnt, t_len * p_cnt, t_len * p_cnt)

    def lift_b(direction, pr, pi):
        br = bb_re[direction].transpose(0, 2, 1)[:, None]
        bi = bb_im[direction].transpose(0, 2, 1)[:, None]
        pr, pi = pr[:, :, None], pi[:, :, None]
        return pr * br - pi * bi, pr * bi + pi * br

    bf_re, bf_im = lift_b(0, pwt_re[0][:, :t_len][:, ::-1], pwt_im[0][:, :t_len][:, ::-1])
    bw_re, bw_im = lift_b(1, pwt_re[1][:, :t_len], pwt_im[1][:, :t_len])
    bbig = jnp.concatenate([bf_re, bw_re, bf_im, bw_im], axis=-1).reshape(g_cnt, t_len * p_cnt, 4 * n_cnt)

    fwd = lambda m: m[0][:, :, 1:t_len + 1]
    bwd = lambda m: m[1][:, :, 1:t_len + 1][:, :, ::-1]
    cbig = jnp.concatenate([fwd(cp_re), bwd(cp_re), -fwd(cp_im), -bwd(cp_im)], axis=1)
    cbig = cbig.reshape(g_cnt, 4 * n_cnt, t_len * p_cnt)

    a_re = jnp.concatenate([pw_re[0][..., t_len], pw_re[1][..., t_len]], axis=-1)
    a_im = jnp.concatenate([pw_im[0][..., t_len], pw_im[1][..., t_len]], axis=-1)
    a16 = jnp.stack([a_re, a_im], axis=1)
    return bbig.astype(BF16), kbig.astype(BF16), cbig.astype(BF16), a16


def _ssm_kernel(u_ref, bbig_ref, kbig_ref, cbig_ref, a_ref, z_ref, v_scr, s_scr, sp_scr, *, bsz, rows_per_dot):
    rows = u_ref.shape[1]
    nc = rows // bsz
    half = 2 * SSM_STATE

    def local_states(i, carry):
        r = pl.multiple_of(i * rows_per_dot, rows_per_dot)
        v_scr[pl.ds(r, rows_per_dot), :] = jnp.dot(
            u_ref[0, pl.ds(r, rows_per_dot), :], bbig_ref[0], preferred_element_type=F32)
        return carry

    lax.fori_loop(0, rows // rows_per_dot, local_states, 0)

    a_re = a_ref[0, 0:1, :]
    a_im = a_ref[0, 1:2, :]
    is_fwd = lax.broadcasted_iota(jnp.int32, (bsz, half), 1) < SSM_STATE

    def scan_step(i, carry):
        s_re, s_im = carry
        rf = pl.multiple_of(i * bsz, bsz)
        rb = pl.multiple_of((nc - 1 - i) * bsz, bsz)
        s_scr[pl.ds(rf, bsz), 0:half] = s_re
        s_scr[pl.ds(rf, bsz), half:2 * half] = s_im
        v_re = jnp.where(is_fwd, v_scr[pl.ds(rf, bsz), 0:half], v_scr[pl.ds(rb, bsz), 0:half])
        v_im = jnp.where(is_fwd, v_scr[pl.ds(rf, bsz), half:2 * half], v_scr[pl.ds(rb, bsz), half:2 * half])
        n_re = a_re * s_re - a_im * s_im + v_re
        n_im = a_re * s_im + a_im * s_re + v_im
        return n_re, n_im

    zero = jnp.zeros((bsz, half), F32)
    lax.fori_loop(0, nc, scan_step, (zero, zero))

    is_fwd2 = lax.broadcasted_iota(jnp.int32, (bsz, 2 * half), 1) % half < SSM_STATE

    def entering_states(c, carry):
        rf = pl.multiple_of(c * bsz, bsz)
        rb = pl.multiple_of((nc - 1 - c) * bsz, bsz)
        sp_scr[pl.ds(rf, bsz), :] = jnp.where(
            is_fwd2, s_scr[pl.ds(rf, bsz), :], s_scr[pl.ds(rb, bsz), :]).astype(BF16)
        return carry

    lax.fori_loop(0, nc, entering_states, 0)

    def outputs(i, carry):
        r = pl.multiple_of(i * rows_per_dot, rows_per_dot)
        y = (jnp.dot(u_ref[0, pl.ds(r, rows_per_dot), :], kbig_ref[0], preferred_element_type=F32)
             + jnp.dot(sp_scr[pl.ds(r, rows_per_dot), :], cbig_ref[0], preferred_element_type=F32))
        z_ref[0, pl.ds(r, rows_per_dot), :] = _gelu(y).astype(BF16)
        return carry

    lax.fori_loop(0, rows // rows_per_dot, outputs, 0)


def _ssm(u_t, bbig, kbig, cbig, a16, bsz):
    g_cnt, rows, width = u_t.shape
    rows_per_dot = min(rows, 512)
    n_state = bbig.shape[-1]
    return pl.pallas_call(
        functools.partial(_ssm_kernel, bsz=bsz, rows_per_dot=rows_per_dot),
        out_shape=jax.ShapeDtypeStruct((g_cnt, rows, width), BF16),
        grid=(g_cnt,),
        in_specs=[pl.BlockSpec((1, rows, width), lambda g: (g, 0, 0)),
                  pl.BlockSpec((1, width, n_state), lambda g: (g, 0, 0)),
                  pl.BlockSpec((1, width, width), lambda g: (g, 0, 0)),
                  pl.BlockSpec((1, n_state, width), lambda g: (g, 0, 0)),
                  pl.BlockSpec((1, 2, n_state // 2), lambda g: (g, 0, 0))],
        out_specs=pl.BlockSpec((1, rows, width), lambda g: (g, 0, 0)),
        scratch_shapes=[pltpu.VMEM((rows, n_state), F32),
                        pltpu.VMEM((rows, n_state), F32),
                        pltpu.VMEM((rows, n_state), BF16)],
        compiler_params=_params("parallel"),
        name="s5_mixer",
    )(u_t, bbig, kbig, cbig, a16)


def _merge_kernel(zt_ref, att_ref, x_ref, wglu_ref, bglu_ref, gssm_ref, gatt_ref, wout_ref, gpost_ref, o_ref,
                  z_scr, *, bsz, batch_major):
    for t1 in range(SSM_CHUNK // GROUPS_PER_VREG):
        for g1 in range(z_scr.shape[1] // LANES):
            vs = [zt_ref[g1 * GROUPS_PER_VREG + g2, :, t1 * LANES:(t1 + 1) * LANES].astype(F32)
                  for g2 in range(GROUPS_PER_VREG)]
            for t2, tile in enumerate(_atom_transpose(vs)):
                t = t1 * GROUPS_PER_VREG + t2
                z_scr[t * bsz:(t + 1) * bsz, g1 * LANES:(g1 + 1) * LANES] = tile
    z = z_scr[...]
    gate = jnp.dot(z.astype(BF16), wglu_ref[...], preferred_element_type=F32) + bglu_ref[...]
    y_ssm = z * jax.nn.sigmoid(gate)
    y_att = att_ref[...].astype(F32)
    merged = jnp.concatenate([_rms(y_ssm, gssm_ref[...]).astype(BF16),
                              _rms(y_att, gatt_ref[...]).astype(BF16)], axis=-1)
    y = jnp.dot(merged, wout_ref[...], preferred_element_type=F32)
    o_ref[...] = _load_stream(x_ref, o_ref.shape[1], batch_major) + _rms(y, gpost_ref[...])


def _merge(z_t, att2d, x2d, w_glu, b_glu, g_ssm, g_att, w_out, g_post, bsz, batch_major):
    d = w_out.shape[1]
    t = x2d.size // d
    g_cnt, nc, _, lanes = z_t.shape
    ws, wa = g_cnt * SSM_GROUP, att2d.shape[1]
    tm = SSM_CHUNK * bsz
    row = lambda w: pl.BlockSpec((tm, w), lambda i: (i, 0))
    full = lambda a: pl.BlockSpec(a.shape, lambda i: (0,) * a.ndim)
    return pl.pallas_call(
        functools.partial(_merge_kernel, bsz=bsz, batch_major=batch_major),
        out_shape=jax.ShapeDtypeStruct((t, d), F32),
        grid=(nc,),
        in_specs=[pl.BlockSpec((g_cnt, None, bsz, lanes), lambda i: (0, i, 0, 0)),
                  row(wa), _stream_spec(tm, d, bsz, batch_major), full(w_glu), full(b_glu), full(g_ssm), full(g_att),
                  full(w_out), full(g_post)],
        out_specs=row(d),
        scratch_shapes=[pltpu.VMEM((tm, ws), F32)],
        compiler_params=_params("parallel"),
        name="merge",
    )(z_t, att2d, x2d, w_glu, b_glu, g_ssm, g_att, w_out, g_post)


def _ffn_kernel(xp_ref, x_ref, xn_ref, gpre_ref, wup_ref, cw_ref, cb_ref, wdown_ref, gpost_ref, o_ref,
                h_scr, a_scr, act_scr, *, d_ff, ff_chunk, halo, out_batch_major):
    i = pl.program_id(0)
    n_tiles = pl.num_programs(0)
    tm = x_ref.shape[0]
    g = gpre_ref[...]
    keep_prev = jnp.where(i == 0, 0.0, 1.0)
    keep_next = jnp.where(i == n_tiles - 1, 0.0, 1.0)
    h_scr[0:halo, :] = (_rms(xp_ref[...], g) * keep_prev).astype(BF16)
    h_scr[halo:halo + tm, :] = _rms(x_ref[...], g).astype(BF16)
    h_scr[halo + tm:, :] = (_rms(xn_ref[...], g) * keep_next).astype(BF16)
    n_chunks = d_ff // ff_chunk

    def cols(base, c):
        return pl.ds(base + c * ff_chunk, ff_chunk)

    def up(c, slot):
        h = h_scr[...]
        for part, base in enumerate((0, d_ff)):
            a_scr[slot, part] = jnp.dot(h, wup_ref[:, cols(base, c)], preferred_element_type=F32)

    def post(c, slot):
        def conv(part, base):
            w = cw_ref[:, cols(base, c)]
            return (a_scr[slot, part, 0:tm] * w[0:1] + a_scr[slot, part, halo:halo + tm] * w[1:2]
                    + a_scr[slot, part, 2 * halo:2 * halo + tm] * w[2:3] + cb_ref[:, cols(base, c)])

        act = _gelu(conv(1, d_ff)) * conv(0, 0)
        act_scr[:, cols(0, c)] = act.astype(BF16)

    n_slots = a_scr.shape[0]
    up(0, 0)
    for c in range(n_chunks - 1):
        up(c + 1, (c + 1) % n_slots)
        post(c, c % n_slots)
    k_head = (n_chunks - 1) * ff_chunk
    y = jnp.dot(act_scr[:, :k_head], wdown_ref[:k_head, :], preferred_element_type=F32)
    post(n_chunks - 1, (n_chunks - 1) % n_slots)
    y = y + jnp.dot(act_scr[:, k_head:], wdown_ref[k_head:, :], preferred_element_type=F32)
    out = x_ref[...] + _rms(y, gpost_ref[...])
    if out_batch_major:
        d = out.shape[1]
        for t in range(tm // halo):
            o_ref[:, t * d:(t + 1) * d] = out[t * halo:(t + 1) * halo]
    else:
        o_ref[...] = out


def _ffn(x2d, g_pre, w_up, conv_w, conv_b, w_down, g_post, bsz, tm, ff_chunk, out_batch_major):
    t, d = x2d.shape
    d_ff = w_down.shape[0]
    halo = bsz
    hb = tm // halo
    n_halo_blocks = t // halo
    full = lambda a: pl.BlockSpec(a.shape, lambda i: (0,) * a.ndim)
    return pl.pallas_call(
        functools.partial(_ffn_kernel, d_ff=d_ff, ff_chunk=ff_chunk, halo=halo, out_batch_major=out_batch_major),
        out_shape=jax.ShapeDtypeStruct((bsz, (t // bsz) * d) if out_batch_major else (t, d), F32),
        grid=(t // tm,),
        in_specs=[pl.BlockSpec((halo, d), lambda i: (jnp.maximum(i * hb - 1, 0), 0)),
                  pl.BlockSpec((tm, d), lambda i: (i, 0)),
                  pl.BlockSpec((halo, d), lambda i: (jnp.minimum((i + 1) * hb, n_halo_blocks - 1), 0)),
                  full(g_pre), full(w_up), full(conv_w), full(conv_b), full(w_down), full(g_post)],
        out_specs=_stream_spec(tm, d, bsz, out_batch_major),
        scratch_shapes=[pltpu.VMEM((tm + 2 * halo, d), BF16),
                        pltpu.VMEM((3, 2, tm + 2 * halo, ff_chunk), F32),
                        pltpu.VMEM((tm, d_ff), BF16)],
        compiler_params=_params("parallel"),
        name="conv_ffn",
    )(x2d, x2d, x2d, g_pre, w_up, conv_w, conv_b, w_down, g_post)


def _tile_sizes(bsz, seq):
    steps = 32 if seq % 32 == 0 else 1
    return steps * bsz, 256


def kernel(x, rel_bias, pre_mix_norm, w_in, lam_re, lam_im, log_step, b_re, b_im, c_re, c_im, ssm_d, w_glu,
           b_glu, attn_sink, ssm_out_norm, attn_out_norm, w_out, post_mix_norm, pre_ffn_norm, w_up, conv_w,
           conv_b, w_down, post_ffn_norm):
    bsz, seq, d_model = x.shape
    depth = w_in.shape[0]
    ssm_width = ssm_d.shape[1]
    attn_width = w_out.shape[1] - ssm_width
    g_cnt = ssm_width // SSM_GROUP
    nc = seq // SSM_CHUNK
    assert seq % BLOCK == 0 and bsz % 16 == 0 and ssm_width % LANES == 0
    assert attn_width == 2 * len(HEAD_ORDER[0]) * HEAD_DIM and rel_bias.shape == (N_BUCKETS, attn_width // HEAD_DIM)
    tm_ffn, ff_chunk = _tile_sizes(bsz, seq)
    assert w_down.shape[1] % ff_chunk == 0

    bias = _bias_table(rel_bias)
    ssm_ops = jax.vmap(_ssm_operators)(lam_re, lam_im, log_step, b_re, b_im, c_re, c_im, ssm_d)
    row = lambda a: a.astype(F32).reshape(1, -1)
    q_scale = jnp.concatenate([jnp.ones((ssm_width,), F32), jnp.full((attn_width,), HEAD_DIM ** -0.5, F32),
                               jnp.ones((w_in.shape[2] - ssm_width - attn_width,), F32)])

    x2d = x.reshape(bsz, seq * d_model)
    for l in range(depth):
        first, last = l == 0, l == depth - 1
        u_t, qkv = _proj(x2d, row(pre_mix_norm[l]), (w_in[l] * q_scale).astype(BF16), bsz, ssm_width, first)
        qkv3 = qkv.reshape(seq, bsz, -1).transpose(1, 0, 2)
        att = _attention(qkv3, bias, attn_sink[l].astype(F32), attn_width)
        att2d = att.transpose(1, 0, 2).reshape(seq * bsz, attn_width)

        z_t = _ssm(u_t.reshape(g_cnt, nc * bsz, -1), *(op[l] for op in ssm_ops), bsz).reshape(u_t.shape)

        x2d = _merge(z_t, att2d, x2d, w_glu[l].astype(BF16), row(b_glu[l]), row(ssm_out_norm[l]),
                     row(attn_out_norm[l]), w_out[l].astype(BF16), row(post_mix_norm[l]), bsz, first)
        x2d = _ffn(x2d, row(pre_ffn_norm[l]), w_up[l].astype(BF16), conv_w[l].astype(F32), row(conv_b[l]),
                   w_down[l].astype(BF16), row(post_ffn_norm[l]), bsz, tm_ffn, ff_chunk, last)
    return x2d.reshape(bsz, seq, d_model)
```

```python
import functools
import math

import numpy as np
import jax
import jax.numpy as jnp
from jax import lax
from jax.experimental import pallas as pl
from jax.experimental.pallas import tpu as pltpu

F32 = jnp.float32
BF16 = jnp.bfloat16

LANES = 128
SSM_GROUP = 16
SSM_STATE = 64
HEAD_DIM = 64
N_KV_HEADS = 2
WINDOW = 128
BLOCK = 128
N_BUCKETS = 32
MAX_DISTANCE = 128
EPS = 1e-6
NEG_INF = -1e30
SSM_CHUNK = 16
Q_BLOCKS = 1
GROUPS_PER_VREG = LANES // SSM_GROUP

VMEM_LIMIT_BYTES = 56 * 1024 * 1024

HEAD_ORDER = ((0, 2, 5, 7), (1, 3, 4, 6))


def _params(*sem):
    return pltpu.CompilerParams(dimension_semantics=sem, vmem_limit_bytes=VMEM_LIMIT_BYTES)


def _rms(x, g):
    return x * lax.rsqrt(jnp.mean(x * x, axis=-1, keepdims=True) + EPS) * g


def _gelu(x):
    c = math.sqrt(2.0 / math.pi)
    return x * (0.5 + 0.5 * jnp.tanh(x * (c + (c * 0.044715) * (x * x))))


def _atom_transpose(vs):
    atom = lax.broadcasted_iota(jnp.int32, vs[0].shape, 1) // SSM_GROUP
    for d in (4, 2, 1):
        keep = (atom & d) == 0
        nxt = list(vs)
        for i in range(GROUPS_PER_VREG):
            if i & d == 0:
                a, b = vs[i], vs[i + d]
                nxt[i] = jnp.where(keep, a, pltpu.roll(b, SSM_GROUP * d, axis=1))
                nxt[i + d] = jnp.where(keep, pltpu.roll(a, LANES - SSM_GROUP * d, axis=1), b)
        vs = nxt
    return vs


def _bucket_table():
    half = N_BUCKETS // 2
    max_exact = half // 2
    qi = np.arange(BLOCK)[:, None]
    sj = np.arange(3 * BLOCK)[None, :]
    rel = sj - BLOCK - qi
    n = np.abs(rel)
    nf = np.maximum(n, 1).astype(np.float64)
    large = max_exact + (np.log(nf / max_exact) / math.log(MAX_DISTANCE / max_exact)
                         * (half - max_exact)).astype(np.int32)
    large = np.minimum(large, half - 1)
    bucket = np.where(rel > 0, half, 0) + np.where(n < max_exact, n, large)
    return np.where(n <= WINDOW, bucket, -1).astype(np.int32)


def _bias_kernel(bucket_ref, rel_bias_ref, out_ref):
    bucket = bucket_ref[...]
    col = lax.broadcasted_iota(jnp.int32, bucket.shape, 1)
    for grp, heads in enumerate(HEAD_ORDER):
        for slot, h in enumerate(heads):
            acc = jnp.full(bucket.shape, NEG_INF, F32)
            for b in range(N_BUCKETS):
                acc = jnp.where(bucket == b, rel_bias_ref[b, h], acc)
            for variant in range(4):
                lo = BLOCK if variant & 1 else 0
                hi = 2 * BLOCK if variant & 2 else 3 * BLOCK
                out_ref[variant, grp, slot * BLOCK:(slot + 1) * BLOCK, :] = jnp.where(
                    (col >= lo) & (col < hi), acc, NEG_INF)


def _bias_table(rel_bias):
    rows = len(HEAD_ORDER[0]) * BLOCK
    bucket = jnp.asarray(_bucket_table())
    return pl.pallas_call(
        _bias_kernel,
        out_shape=jax.ShapeDtypeStruct((4, len(HEAD_ORDER), rows, 3 * BLOCK), F32),
        in_specs=[pl.BlockSpec(memory_space=pltpu.VMEM), pl.BlockSpec(memory_space=pltpu.SMEM)],
        out_specs=pl.BlockSpec(memory_space=pltpu.VMEM),
        name="bias_table",
    )(bucket, rel_bias.astype(F32))


def _stream_spec(tm, d, bsz, batch_major):
    if batch_major:
        return pl.BlockSpec((bsz, (tm // bsz) * d), lambda i: (0, i))
    return pl.BlockSpec((tm, d), lambda i: (i, 0))


def _load_stream(x_ref, d, batch_major):
    if not batch_major:
        return x_ref[...]
    return jnp.concatenate([x_ref[:, t * d:(t + 1) * d] for t in range(x_ref.shape[1] // d)], axis=0)


def _proj_kernel(x_ref, g_ref, w_ref, ut_ref, qkv_ref, *, bsz, ssm_width, batch_major):
    h = _rms(_load_stream(x_ref, g_ref.shape[1], batch_major), g_ref[...]).astype(BF16)
    r = jnp.dot(h, w_ref[...], preferred_element_type=F32)
    qkv_ref[...] = r[:, ssm_width:].astype(BF16)
    for t1 in range(SSM_CHUNK // GROUPS_PER_VREG):
        for g1 in range(ssm_width // LANES):
            vs = [r[(t1 * GROUPS_PER_VREG + t2) * bsz:(t1 * GROUPS_PER_VREG + t2 + 1) * bsz,
                    g1 * LANES:(g1 + 1) * LANES] for t2 in range(GROUPS_PER_VREG)]
            for g2, tile in enumerate(_atom_transpose(vs)):
                ut_ref[g1 * GROUPS_PER_VREG + g2, :, t1 * LANES:(t1 + 1) * LANES] = tile.astype(BF16)


def _proj(x2d, g, w, bsz, ssm_width, batch_major):
    d, n = w.shape
    t = x2d.size // d
    tm = SSM_CHUNK * bsz
    nc = t // tm
    g_cnt = ssm_width // SSM_GROUP
    lanes = SSM_CHUNK * SSM_GROUP
    return pl.pallas_call(
        functools.partial(_proj_kernel, bsz=bsz, ssm_width=ssm_width, batch_major=batch_major),
        out_shape=(jax.ShapeDtypeStruct((g_cnt, nc, bsz, lanes), BF16),
                   jax.ShapeDtypeStruct((t, n - ssm_width), BF16)),
        grid=(nc,),
        in_specs=[_stream_spec(tm, d, bsz, batch_major),
                  pl.BlockSpec((1, d), lambda i: (0, 0)),
                  pl.BlockSpec((d, n), lambda i: (0, 0))],
        out_specs=(pl.BlockSpec((g_cnt, None, bsz, lanes), lambda i: (0, i, 0, 0)),
                   pl.BlockSpec((tm, n - ssm_width), lambda i: (i, 0))),
        compiler_params=_params("parallel"),
        name="proj",
    )(x2d, g, w)


def _attn_kernel(qlo_ref, qhi_ref, *refs):
    n_kb = Q_BLOCKS + 2
    k_refs, v_refs = refs[:n_kb], refs[n_kb:2 * n_kb]
    bias_ref, sink_ref, o_ref = refs[2 * n_kb:]
    j = pl.program_id(1)
    nj = pl.num_programs(1)
    k = jnp.concatenate([r[...] for r in k_refs], axis=0)
    v = jnp.concatenate([r[...] for r in v_refs], axis=0)
    swap = lambda a: jnp.concatenate([a[:, HEAD_DIM:], a[:, :HEAD_DIM]], axis=1)
    k_sw, v_sw = swap(k), swap(v)
    low = lax.broadcasted_iota(jnp.int32, k.shape, 1) < HEAD_DIM
    zero = jnp.zeros_like(k)
    k_even = (jnp.where(low, k, zero), jnp.where(low, k_sw, zero))
    k_odd = (jnp.where(low, zero, k_sw), jnp.where(low, zero, k))
    ones = jnp.ones_like(v)
    v_ext = (jnp.concatenate([v, ones], axis=1), jnp.concatenate([v_sw, ones], axis=1))
    qk = lambda a, b: lax.dot_general(a, b, (((1,), (1,)), ((), ())), preferred_element_type=F32)
    low_o = lax.broadcasted_iota(jnp.int32, (BLOCK, LANES), 1) < HEAD_DIM
    for sub in range(Q_BLOCKS):
        variant = jnp.where(j == 0, 1, 0) if sub == 0 else 0
        variant = variant + (jnp.where(j == nj - 1, 2, 0) if sub == Q_BLOCKS - 1 else 0)
        qrows = slice(sub * BLOCK, (sub + 1) * BLOCK)
        keys = lambda a: a[sub * BLOCK:(sub + 3) * BLOCK]
        q_kv = (jnp.concatenate([qlo_ref[qrows, 0:LANES], qlo_ref[qrows, LANES:2 * LANES]], axis=0),
                jnp.concatenate([qhi_ref[qrows, 0:LANES], qhi_ref[qrows, LANES:2 * LANES]], axis=0))
        scores = (jnp.concatenate([qk(q_kv[0], keys(k_even[0])), qk(q_kv[1], keys(k_odd[1]))], axis=0),
                  jnp.concatenate([qk(q_kv[0], keys(k_odd[0])), qk(q_kv[1], keys(k_even[1]))], axis=0))
        outs = []
        for grp, heads in enumerate(HEAD_ORDER):
            sink = jnp.concatenate([jnp.full((BLOCK, LANES), sink_ref[h], F32) for h in heads], axis=0)
            s = scores[grp] + bias_ref[variant, grp]
            m = jnp.maximum(jnp.broadcast_to(jnp.max(s, axis=-1, keepdims=True), sink.shape), sink)
            e = jnp.concatenate([jnp.exp(s[:, c * LANES:(c + 1) * LANES] - m)
                                 for c in range(s.shape[1] // LANES)], axis=1).astype(BF16)
            o = jnp.dot(e, keys(v_ext[grp]), preferred_element_type=F32)
            outs.append(o[:, :LANES] / (o[:, LANES:] + jnp.exp(sink - m)))
        o_a, o_b = outs
        rows = lambda a, slot: a[slot * BLOCK:(slot + 1) * BLOCK]
        pairs = [jnp.where(low_o, rows(o_a, 0), rows(o_b, 0)), jnp.where(low_o, rows(o_a, 1), rows(o_b, 1)),
                 jnp.where(low_o, rows(o_b, 2), rows(o_a, 2)), jnp.where(low_o, rows(o_b, 3), rows(o_a, 3))]
        o_ref[qrows, :] = jnp.concatenate(pairs, axis=1).astype(BF16)


def _attention(qkv3, bias, sink, attn_width):
    bsz, seq, width = qkv3.shape
    kv_width = N_KV_HEADS * HEAD_DIM
    q_half = attn_width // 2
    assert width == attn_width + 2 * kv_width and seq % (Q_BLOCKS * BLOCK) == 0
    nb = seq // BLOCK
    rows = Q_BLOCKS * BLOCK
    q_spec = lambda half: pl.BlockSpec((None, rows, q_half), lambda b, j: (b, j, half))
    kv_spec = lambda col, off: pl.BlockSpec(
        (None, BLOCK, kv_width), lambda b, j: (b, jnp.clip(j * Q_BLOCKS + off, 0, nb - 1), col))
    k_col = attn_width // kv_width
    offsets = range(-1, Q_BLOCKS + 1)
    return pl.pallas_call(
        _attn_kernel,
        out_shape=jax.ShapeDtypeStruct((bsz, seq, attn_width), BF16),
        grid=(bsz, nb // Q_BLOCKS),
        in_specs=[q_spec(0), q_spec(1)]
                 + [kv_spec(k_col, off) for off in offsets] + [kv_spec(k_col + 1, off) for off in offsets]
                 + [pl.BlockSpec(bias.shape, lambda b, j: (0, 0, 0, 0)), pl.BlockSpec(memory_space=pltpu.SMEM)],
        out_specs=pl.BlockSpec((None, rows, attn_width), lambda b, j: (b, j, 0)),
        compiler_params=_params("parallel", "parallel"),
        name="attention",
    )(*([qkv3] * (2 + 2 * len(offsets))), bias, sink)


def _ssm_operators(lam_re, lam_im, log_step, b_re, b_im, c_re, c_im, d):
    hp = lax.Precision.HIGH
    t_len = SSM_CHUNK
    lam_re, lam_im = lam_re.astype(F32), lam_im.astype(F32)
    dt = jnp.exp(log_step.astype(F32))[..., None]
    ld_re, ld_im = lam_re * dt, lam_im * dt
    mag = jnp.exp(ld_re)
    lb_re, lb_im = mag * jnp.cos(ld_im), mag * jnp.sin(ld_im)
    den = lam_re * lam_re + lam_im * lam_im
    nr, ni = lb_re - 1.0, lb_im
    coef_re = (nr * lam_re + ni * lam_im) / den
    coef_im = (ni * lam_re - nr * lam_im) / den
    b_re, b_im = b_re.astype(F32), b_im.astype(F32)
    bb_re = coef_re[..., None] * b_re - coef_im[..., None] * b_im
    bb_im = coef_re[..., None] * b_im + coef_im[..., None] * b_re
    g_cnt, n_cnt, p_cnt = bb_re.shape[1:]

    def powers(re, im, tau):
        m = jnp.exp(re * tau)
        return m * jnp.cos(im * tau), m * jnp.sin(im * tau)

    tau = jnp.arange(t_len + 1, dtype=F32)
    pw_re, pw_im = powers(ld_re[..., None], ld_im[..., None], tau)
    pwt_re, pwt_im = powers(ld_re[:, :, None], ld_im[:, :, None], tau[:, None])

    lanes = t_len * p_cnt
    ct_re = jnp.tile(c_re.astype(F32).transpose(0, 1, 3, 2), (1, 1, 1, t_len))
    ct_im = jnp.tile(c_im.astype(F32).transpose(0, 1, 3, 2), (1, 1, 1, t_len))

    def c_times(direction, pr, pi):
        pr, pi = jnp.repeat(pr, p_cnt, axis=-1), jnp.repeat(pi, p_cnt, axis=-1)
        cr, ci = ct_re[direction], ct_im[direction]
        return cr * pr - ci * pi, cr * pi + ci * pr

    k_lag = []
    for direction in range(2):
        cp_re, cp_im = c_times(direction, pw_re[direction][..., :t_len], pw_im[direction][..., :t_len])
        k_lag.append(jnp.einsum('gnq,gnm->gqm',
                                jnp.concatenate([bb_re[direction], -bb_im[direction]], axis=1),
                                jnp.concatenate([cp_re, cp_im], axis=1), precision=hp))
    skip = d.astype(F32).reshape(g_cnt, 1, p_cnt) * jnp.eye(p_cnt, dtype=F32)
    k_fwd = k_lag[0].at[:, :, :p_cnt].add(skip)
    k_bwd_rev = k_lag[1].reshape(g_cnt, p_cnt, t_len, p_cnt)[:, :, ::-1].reshape(g_cnt, p_cnt, lanes)
    zeros = jnp.zeros_like(k_fwd)
    f_pad = jnp.concatenate([zeros, k_fwd], axis=2)
    b_pad = jnp.concatenate([k_bwd_rev, zeros], axis=2)
    kbig = jnp.stack([f_pad[:, :, (t_len - s) * p_cnt:(2 * t_len - s) * p_cnt]
                      + b_pad[:, :, (t_len - 1 - s) * p_cnt:(2 * t_len - 1 - s) * p_cnt]
                      for s in range(t_len)], axis=1)
    kbig = kbig.reshape(g_cnt, lanes, lanes)

    def lift_b(direction, pr, pi):
        br = jnp.tile(bb_re[direction].transpose(0, 2, 1), (1, t_len, 1))
        bi = jnp.tile(bb_im[direction].transpose(0, 2, 1), (1, t_len, 1))
        pr, pi = jnp.repeat(pr, p_cnt, axis=1), jnp.repeat(pi, p_cnt, axis=1)
        return pr * br - pi * bi, pr * bi + pi * br

    bf_re, bf_im = lift_b(0, pwt_re[0][:, :t_len][:, ::-1], pwt_im[0][:, :t_len][:, ::-1])
    bw_re, bw_im = lift_b(1, pwt_re[1][:, :t_len], pwt_im[1][:, :t_len])
    bbig = jnp.concatenate([bf_re, bw_re, bf_im, bw_im], axis=-1)

    cf_re, cf_im = c_times(0, pw_re[0][..., 1:], pw_im[0][..., 1:])
    cb_re, cb_im = c_times(1, pw_re[1][..., :0:-1], pw_im[1][..., :0:-1])
    cbig = jnp.concatenate([cf_re, cb_re, -cf_im, -cb_im], axis=1)

    a_re = jnp.concatenate([pw_re[0][..., t_len], pw_re[1][..., t_len]], axis=-1)
    a_im = jnp.concatenate([pw_im[0][..., t_len], pw_im[1][..., t_len]], axis=-1)
    a16 = jnp.stack([a_re, a_im], axis=1)
    return bbig.astype(BF16), kbig.astype(BF16), cbig.astype(BF16), a16


def _ssm_kernel(u_ref, bbig_ref, kbig_ref, cbig_ref, a_ref, z_ref, v_scr, s_scr, *, bsz, rows_per_dot):
    rows = u_ref.shape[1]
    nc = rows // bsz
    half = 2 * SSM_STATE
    blocks = [slice(r, r + rows_per_dot) for r in range(0, rows, rows_per_dot)]

    for blk in blocks:
        v_scr[blk, :] = jnp.dot(u_ref[0, blk, :], bbig_ref[0], preferred_element_type=F32)

    a_re = a_ref[0, 0:1, :]
    a_im = a_ref[0, 1:2, :]
    is_fwd = lax.broadcasted_iota(jnp.int32, (bsz, half), 1) < SSM_STATE

    def scan_step(i, carry):
        s_re, s_im = carry
        rf = pl.multiple_of(i * bsz, bsz)
        rb = pl.multiple_of((nc - 1 - i) * bsz, bsz)
        s_scr[pl.ds(rf, bsz), 0:half] = s_re
        s_scr[pl.ds(rf, bsz), half:2 * half] = s_im
        v_re = jnp.where(is_fwd, v_scr[pl.ds(rf, bsz), 0:half], v_scr[pl.ds(rb, bsz), 0:half])
        v_im = jnp.where(is_fwd, v_scr[pl.ds(rf, bsz), half:2 * half], v_scr[pl.ds(rb, bsz), half:2 * half])
        n_re = a_re * s_re - a_im * s_im + v_re
        n_im = a_re * s_im + a_im * s_re + v_im
        return n_re, n_im

    zero = jnp.zeros((bsz, half), F32)
    lax.fori_loop(0, nc, scan_step, (zero, zero), unroll=2)

    is_fwd2 = lax.broadcasted_iota(jnp.int32, (bsz, 2 * half), 1) % half < SSM_STATE
    chunk = lambda c: s_scr[c * bsz:(c + 1) * bsz, :]
    for blk in blocks:
        entering = jnp.concatenate([jnp.where(is_fwd2, chunk(c), chunk(nc - 1 - c))
                                    for c in range(blk.start // bsz, blk.stop // bsz)], axis=0)
        y = (jnp.dot(u_ref[0, blk, :], kbig_ref[0], preferred_element_type=F32)
             + jnp.dot(entering.astype(BF16), cbig_ref[0], preferred_element_type=F32))
        z_ref[0, blk, :] = _gelu(y).astype(BF16)


def _ssm(u_t, bbig, kbig, cbig, a16, bsz):
    g_cnt, rows, width = u_t.shape
    rows_per_dot = min(rows, 512)
    n_state = bbig.shape[-1]
    return pl.pallas_call(
        functools.partial(_ssm_kernel, bsz=bsz, rows_per_dot=rows_per_dot),
        out_shape=jax.ShapeDtypeStruct((g_cnt, rows, width), BF16),
        grid=(g_cnt,),
        in_specs=[pl.BlockSpec((1, rows, width), lambda g: (g, 0, 0)),
                  pl.BlockSpec((1, width, n_state), lambda g: (g, 0, 0)),
                  pl.BlockSpec((1, width, width), lambda g: (g, 0, 0)),
                  pl.BlockSpec((1, n_state, width), lambda g: (g, 0, 0)),
                  pl.BlockSpec((1, 2, n_state // 2), lambda g: (g, 0, 0))],
        out_specs=pl.BlockSpec((1, rows, width), lambda g: (g, 0, 0)),
        scratch_shapes=[pltpu.VMEM((rows, n_state), F32),
                        pltpu.VMEM((rows, n_state), F32)],
        compiler_params=_params("parallel"),
        name="s5_mixer",
    )(u_t, bbig, kbig, cbig, a16)


def _merge_kernel(zt_ref, att_ref, x_ref, wglu_ref, bglu_ref, gssm_ref, gatt_ref, wout_ref, gpost_ref, o_ref,
                  z_scr, *, bsz, batch_major):
    for t1 in range(SSM_CHUNK // GROUPS_PER_VREG):
        for g1 in range(z_scr.shape[1] // LANES):
            vs = [zt_ref[g1 * GROUPS_PER_VREG + g2, :, t1 * LANES:(t1 + 1) * LANES].astype(F32)
                  for g2 in range(GROUPS_PER_VREG)]
            for t2, tile in enumerate(_atom_transpose(vs)):
                t = t1 * GROUPS_PER_VREG + t2
                z_scr[t * bsz:(t + 1) * bsz, g1 * LANES:(g1 + 1) * LANES] = tile
    z = z_scr[...]
    gate = jnp.dot(z.astype(BF16), wglu_ref[...], preferred_element_type=F32) + bglu_ref[...]
    y_ssm = z * jax.nn.sigmoid(gate)
    y_att = att_ref[...].astype(F32)
    merged = jnp.concatenate([_rms(y_ssm, gssm_ref[...]).astype(BF16),
                              _rms(y_att, gatt_ref[...]).astype(BF16)], axis=-1)
    y = jnp.dot(merged, wout_ref[...], preferred_element_type=F32)
    o_ref[...] = _load_stream(x_ref, o_ref.shape[1], batch_major) + _rms(y, gpost_ref[...])


def _merge(z_t, att2d, x2d, w_glu, b_glu, g_ssm, g_att, w_out, g_post, bsz, batch_major):
    d = w_out.shape[1]
    t = x2d.size // d
    g_cnt, nc, _, lanes = z_t.shape
    ws, wa = g_cnt * SSM_GROUP, att2d.shape[1]
    tm = SSM_CHUNK * bsz
    row = lambda w: pl.BlockSpec((tm, w), lambda i: (i, 0))
    full = lambda a: pl.BlockSpec(a.shape, lambda i: (0,) * a.ndim)
    return pl.pallas_call(
        functools.partial(_merge_kernel, bsz=bsz, batch_major=batch_major),
        out_shape=jax.ShapeDtypeStruct((t, d), F32),
        grid=(nc,),
        in_specs=[pl.BlockSpec((g_cnt, None, bsz, lanes), lambda i: (0, i, 0, 0)),
                  row(wa), _stream_spec(tm, d, bsz, batch_major), full(w_glu), full(b_glu), full(g_ssm), full(g_att),
                  full(w_out), full(g_post)],
        out_specs=row(d),
        scratch_shapes=[pltpu.VMEM((tm, ws), F32)],
        compiler_params=_params("parallel"),
        name="merge",
    )(z_t, att2d, x2d, w_glu, b_glu, g_ssm, g_att, w_out, g_post)


def _ffn_kernel(xp_ref, x_ref, xn_ref, gpre_ref, wup_ref, cw_ref, cb_ref, wdown_ref, gpost_ref, o_ref,
                h_scr, a_scr, act_scr, *, d_ff, ff_chunk, halo, out_batch_major):
    i = pl.program_id(0)
    n_tiles = pl.num_programs(0)
    tm = x_ref.shape[0]
    g = gpre_ref[...]
    keep_prev = jnp.where(i == 0, 0.0, 1.0)
    keep_next = jnp.where(i == n_tiles - 1, 0.0, 1.0)
    h_scr[0:halo, :] = (_rms(xp_ref[...], g) * keep_prev).astype(BF16)
    h_scr[halo:halo + tm, :] = _rms(x_ref[...], g).astype(BF16)
    h_scr[halo + tm:, :] = (_rms(xn_ref[...], g) * keep_next).astype(BF16)
    n_chunks = d_ff // ff_chunk

    def cols(base, c):
        return pl.ds(base + c * ff_chunk, ff_chunk)

    def up(c, slot):
        h = h_scr[...]
        for part, base in enumerate((0, d_ff)):
            a_scr[slot, part] = jnp.dot(h, wup_ref[:, cols(base, c)], preferred_element_type=F32)

    def post(c, slot):
        def conv(part, base):
            w = cw_ref[:, cols(base, c)]
            return (a_scr[slot, part, 0:tm] * w[0:1] + a_scr[slot, part, halo:halo + tm] * w[1:2]
                    + a_scr[slot, part, 2 * halo:2 * halo + tm] * w[2:3] + cb_ref[:, cols(base, c)])

        act = _gelu(conv(1, d_ff)) * conv(0, 0)
        act_scr[:, cols(0, c)] = act.astype(BF16)

    n_slots = a_scr.shape[0]
    up(0, 0)
    for c in range(n_chunks - 1):
        up(c + 1, (c + 1) % n_slots)
        post(c, c % n_slots)
    k_head = (n_chunks - 1) * ff_chunk
    y = jnp.dot(act_scr[:, :k_head], wdown_ref[:k_head, :], preferred_element_type=F32)
    post(n_chunks - 1, (n_chunks - 1) % n_slots)
    y = y + jnp.dot(act_scr[:, k_head:], wdown_ref[k_head:, :], preferred_element_type=F32)
    out = x_ref[...] + _rms(y, gpost_ref[...])
    if out_batch_major:
        d = out.shape[1]
        for t in range(tm // halo):
            o_ref[:, t * d:(t + 1) * d] = out[t * halo:(t + 1) * halo]
    else:
        o_ref[...] = out


def _ffn(x2d, g_pre, w_up, conv_w, conv_b, w_down, g_post, bsz, tm, ff_chunk, out_batch_major):
    t, d = x2d.shape
    d_ff = w_down.shape[0]
    halo = bsz
    hb = tm // halo
    n_halo_blocks = t // halo
    full = lambda a: pl.BlockSpec(a.shape, lambda i: (0,) * a.ndim)
    return pl.pallas_call(
        functools.partial(_ffn_kernel, d_ff=d_ff, ff_chunk=ff_chunk, halo=halo, out_batch_major=out_batch_major),
        out_shape=jax.ShapeDtypeStruct((bsz, (t // bsz) * d) if out_batch_major else (t, d), F32),
        grid=(t // tm,),
        in_specs=[pl.BlockSpec((halo, d), lambda i: (jnp.maximum(i * hb - 1, 0), 0)),
                  pl.BlockSpec((tm, d), lambda i: (i, 0)),
                  pl.BlockSpec((halo, d), lambda i: (jnp.minimum((i + 1) * hb, n_halo_blocks - 1), 0)),
                  full(g_pre), full(w_up), full(conv_w), full(conv_b), full(w_down), full(g_post)],
        out_specs=_stream_spec(tm, d, bsz, out_batch_major),
        scratch_shapes=[pltpu.VMEM((tm + 2 * halo, d), BF16),
                        pltpu.VMEM((3, 2, tm + 2 * halo, ff_chunk), F32),
                        pltpu.VMEM((tm, d_ff), BF16)],
        compiler_params=_params("parallel"),
        name="conv_ffn",
    )(x2d, x2d, x2d, g_pre, w_up, conv_w, conv_b, w_down, g_post)


def _tile_sizes(bsz, seq):
    steps = 32 if seq % 32 == 0 else 1
    return steps * bsz, 256


def kernel(x, rel_bias, pre_mix_norm, w_in, lam_re, lam_im, log_step, b_re, b_im, c_re, c_im, ssm_d, w_glu,
           b_glu, attn_sink, ssm_out_norm, attn_out_norm, w_out, post_mix_norm, pre_ffn_norm, w_up, conv_w,
           conv_b, w_down, post_ffn_norm):
    bsz, seq, d_model = x.shape
    depth = w_in.shape[0]
    ssm_width = ssm_d.shape[1]
    attn_width = w_out.shape[1] - ssm_width
    g_cnt = ssm_width // SSM_GROUP
    nc = seq // SSM_CHUNK
    assert seq % BLOCK == 0 and bsz % 16 == 0 and ssm_width % LANES == 0
    assert attn_width == 2 * len(HEAD_ORDER[0]) * HEAD_DIM and rel_bias.shape == (N_BUCKETS, attn_width // HEAD_DIM)
    tm_ffn, ff_chunk = _tile_sizes(bsz, seq)
    assert w_down.shape[1] % ff_chunk == 0

    bias = _bias_table(rel_bias)
    ssm_ops = jax.vmap(_ssm_operators)(lam_re, lam_im, log_step, b_re, b_im, c_re, c_im, ssm_d)
    row = lambda a: a.astype(F32).reshape(1, -1)
    q_scale = jnp.concatenate([jnp.ones((ssm_width,), F32), jnp.full((attn_width,), HEAD_DIM ** -0.5, F32),
                               jnp.ones((w_in.shape[2] - ssm_width - attn_width,), F32)])

    x2d = x.reshape(bsz, seq * d_model)
    for l in range(depth):
        first, last = l == 0, l == depth - 1
        u_t, qkv = _proj(x2d, row(pre_mix_norm[l]), (w_in[l] * q_scale).astype(BF16), bsz, ssm_width, first)
        qkv3 = qkv.reshape(seq, bsz, -1).transpose(1, 0, 2)
        att = _attention(qkv3, bias, attn_sink[l].astype(F32), attn_width)
        att2d = att.transpose(1, 0, 2).reshape(seq * bsz, attn_width)

        z_t = _ssm(u_t.reshape(g_cnt, nc * bsz, -1), *(op[l] for op in ssm_ops), bsz).reshape(u_t.shape)

        x2d = _merge(z_t, att2d, x2d, w_glu[l].astype(BF16), row(b_glu[l]), row(ssm_out_norm[l]),
                     row(attn_out_norm[l]), w_out[l].astype(BF16), row(post_mix_norm[l]), bsz, first)
        x2d = _ffn(x2d, row(pre_ffn_norm[l]), w_up[l].astype(BF16), conv_w[l].astype(F32), row(conv_b[l]),
                   w_down[l].astype(BF16), row(post_ffn_norm[l]), bsz, tm_ffn, ff_chunk, last)
    return x2d.reshape(bsz, seq, d_model)
```

```python
import functools
import math

import numpy as np
import jax
import jax.numpy as jnp
from jax import lax
from jax.experimental import pallas as pl
from jax.experimental.pallas import tpu as pltpu

F32 = jnp.float32
BF16 = jnp.bfloat16

LANES = 128
SSM_GROUP = 16
SSM_STATE = 64
HEAD_DIM = 64
N_KV_HEADS = 2
WINDOW = 128
BLOCK = 128
N_BUCKETS = 32
MAX_DISTANCE = 128
EPS = 1e-6
NEG_INF = -1e30
SSM_CHUNK = 16
Q_BLOCKS = 1
GROUPS_PER_VREG = LANES // SSM_GROUP

VMEM_LIMIT_BYTES = 56 * 1024 * 1024

HEAD_ORDER = ((0, 2, 5, 7), (1, 3, 4, 6))


def _params(*sem):
    return pltpu.CompilerParams(dimension_semantics=sem, vmem_limit_bytes=VMEM_LIMIT_BYTES)


def _rms(x, g=None):
    y = x * lax.rsqrt(jnp.mean(x * x, axis=-1, keepdims=True) + EPS)
    return y if g is None else y * g


def _gelu(x):
    c = math.sqrt(2.0 / math.pi)
    return x * (0.5 + 0.5 * jnp.tanh(x * (c + (c * 0.044715) * (x * x))))


def _atom_transpose(vs):
    atom = lax.broadcasted_iota(jnp.int32, vs[0].shape, 1) // SSM_GROUP
    for d in (4, 2, 1):
        keep = (atom & d) == 0
        nxt = list(vs)
        for i in range(GROUPS_PER_VREG):
            if i & d == 0:
                a, b = vs[i], vs[i + d]
                nxt[i] = jnp.where(keep, a, pltpu.roll(b, SSM_GROUP * d, axis=1))
                nxt[i + d] = jnp.where(keep, pltpu.roll(a, LANES - SSM_GROUP * d, axis=1), b)
        vs = nxt
    return vs


def _bucket_table():
    half = N_BUCKETS // 2
    max_exact = half // 2
    qi = np.arange(BLOCK)[:, None]
    sj = np.arange(3 * BLOCK)[None, :]
    rel = sj - BLOCK - qi
    n = np.abs(rel)
    nf = np.maximum(n, 1).astype(np.float64)
    large = max_exact + (np.log(nf / max_exact) / math.log(MAX_DISTANCE / max_exact)
                         * (half - max_exact)).astype(np.int32)
    large = np.minimum(large, half - 1)
    bucket = np.where(rel > 0, half, 0) + np.where(n < max_exact, n, large)
    return np.where(n <= WINDOW, bucket, -1).astype(np.int32)


def _bias_kernel(bucket_ref, rel_bias_ref, out_ref):
    bucket = bucket_ref[...]
    col = lax.broadcasted_iota(jnp.int32, bucket.shape, 1)
    for grp, heads in enumerate(HEAD_ORDER):
        for slot, h in enumerate(heads):
            acc = jnp.full(bucket.shape, NEG_INF, F32)
            for b in range(N_BUCKETS):
                acc = jnp.where(bucket == b, rel_bias_ref[b, h], acc)
            for variant in range(4):
                lo = BLOCK if variant & 1 else 0
                hi = 2 * BLOCK if variant & 2 else 3 * BLOCK
                out_ref[variant, grp, slot * BLOCK:(slot + 1) * BLOCK, :] = jnp.where(
                    (col >= lo) & (col < hi), acc, NEG_INF)


def _bias_table(rel_bias):
    rows = len(HEAD_ORDER[0]) * BLOCK
    bucket = jnp.asarray(_bucket_table())
    return pl.pallas_call(
        _bias_kernel,
        out_shape=jax.ShapeDtypeStruct((4, len(HEAD_ORDER), rows, 3 * BLOCK), F32),
        in_specs=[pl.BlockSpec(memory_space=pltpu.VMEM), pl.BlockSpec(memory_space=pltpu.SMEM)],
        out_specs=pl.BlockSpec(memory_space=pltpu.VMEM),
        name="bias_table",
    )(bucket, rel_bias.astype(F32))


def _stream_spec(tm, d, bsz, batch_major):
    if batch_major:
        return pl.BlockSpec((bsz, (tm // bsz) * d), lambda i: (0, i))
    return pl.BlockSpec((tm, d), lambda i: (i, 0))


def _load_stream(x_ref, d, batch_major):
    if not batch_major:
        return x_ref[...]
    return jnp.concatenate([x_ref[:, t * d:(t + 1) * d] for t in range(x_ref.shape[1] // d)], axis=0)


def _proj_kernel(x_ref, w_ref, ut_ref, qkv_ref, *, bsz, ssm_width, batch_major):
    h = _rms(_load_stream(x_ref, w_ref.shape[0], batch_major)).astype(BF16)
    r = jnp.dot(h, w_ref[...], preferred_element_type=F32)
    qkv_ref[...] = r[:, ssm_width:].astype(BF16)
    for t1 in range(SSM_CHUNK // GROUPS_PER_VREG):
        for g1 in range(ssm_width // LANES):
            vs = [r[(t1 * GROUPS_PER_VREG + t2) * bsz:(t1 * GROUPS_PER_VREG + t2 + 1) * bsz,
                    g1 * LANES:(g1 + 1) * LANES] for t2 in range(GROUPS_PER_VREG)]
            for g2, tile in enumerate(_atom_transpose(vs)):
                ut_ref[g1 * GROUPS_PER_VREG + g2, :, t1 * LANES:(t1 + 1) * LANES] = tile.astype(BF16)


def _proj(x2d, w, bsz, ssm_width, batch_major):
    d, n = w.shape
    t = x2d.size // d
    tm = SSM_CHUNK * bsz
    nc = t // tm
    g_cnt = ssm_width // SSM_GROUP
    lanes = SSM_CHUNK * SSM_GROUP
    return pl.pallas_call(
        functools.partial(_proj_kernel, bsz=bsz, ssm_width=ssm_width, batch_major=batch_major),
        out_shape=(jax.ShapeDtypeStruct((g_cnt, nc, bsz, lanes), BF16),
                   jax.ShapeDtypeStruct((t, n - ssm_width), BF16)),
        grid=(nc,),
        in_specs=[_stream_spec(tm, d, bsz, batch_major),
                  pl.BlockSpec((d, n), lambda i: (0, 0))],
        out_specs=(pl.BlockSpec((g_cnt, None, bsz, lanes), lambda i: (0, i, 0, 0)),
                   pl.BlockSpec((tm, n - ssm_width), lambda i: (i, 0))),
        compiler_params=_params("parallel"),
        name="proj",
    )(x2d, w)


def _attn_kernel(q_ref, *refs):
    n_kb = Q_BLOCKS + 2
    kv_refs = refs[:n_kb]
    bias_ref, sink_ref, o_ref = refs[n_kb:]
    j = pl.program_id(1)
    nj = pl.num_programs(1)
    kv_width = N_KV_HEADS * HEAD_DIM
    k = jnp.concatenate([r[:, :kv_width] for r in kv_refs], axis=0)
    v = jnp.concatenate([r[:, kv_width:] for r in kv_refs], axis=0)
    swap = lambda a: jnp.concatenate([a[:, HEAD_DIM:], a[:, :HEAD_DIM]], axis=1)
    k_sw, v_sw = swap(k), swap(v)
    low = lax.broadcasted_iota(jnp.int32, k.shape, 1) < HEAD_DIM
    zero = jnp.zeros_like(k)
    k_even = (jnp.where(low, k, zero), jnp.where(low, k_sw, zero))
    k_odd = (jnp.where(low, zero, k_sw), jnp.where(low, zero, k))
    ones = jnp.ones_like(v)
    v_ext = (jnp.concatenate([v, ones], axis=1), jnp.concatenate([v_sw, ones], axis=1))
    qk = lambda a, b: lax.dot_general(a, b, (((1,), (1,)), ((), ())), preferred_element_type=F32)
    low_o = lax.broadcasted_iota(jnp.int32, (BLOCK, LANES), 1) < HEAD_DIM
    for sub in range(Q_BLOCKS):
        variant = jnp.where(j == 0, 1, 0) if sub == 0 else 0
        variant = variant + (jnp.where(j == nj - 1, 2, 0) if sub == Q_BLOCKS - 1 else 0)
        qrows = slice(sub * BLOCK, (sub + 1) * BLOCK)
        keys = lambda a: a[sub * BLOCK:(sub + 3) * BLOCK]
        q_kv = (jnp.concatenate([q_ref[qrows, 0:LANES], q_ref[qrows, LANES:2 * LANES]], axis=0),
                jnp.concatenate([q_ref[qrows, 2 * LANES:3 * LANES], q_ref[qrows, 3 * LANES:4 * LANES]], axis=0))
        scores = (jnp.concatenate([qk(q_kv[0], keys(k_even[0])), qk(q_kv[1], keys(k_odd[1]))], axis=0),
                  jnp.concatenate([qk(q_kv[0], keys(k_odd[0])), qk(q_kv[1], keys(k_even[1]))], axis=0))
        outs = []
        for grp, heads in enumerate(HEAD_ORDER):
            sink = jnp.concatenate([jnp.full((BLOCK, LANES), sink_ref[h], F32) for h in heads], axis=0)
            s = scores[grp] + bias_ref[variant, grp]
            m = jnp.maximum(jnp.broadcast_to(jnp.max(s, axis=-1, keepdims=True), sink.shape), sink)
            e = jnp.concatenate([jnp.exp(s[:, c * LANES:(c + 1) * LANES] - m)
                                 for c in range(s.shape[1] // LANES)], axis=1).astype(BF16)
            o = jnp.dot(e, keys(v_ext[grp]), preferred_element_type=F32)
            outs.append(o[:, :LANES] / (o[:, LANES:] + jnp.exp(sink - m)))
        o_a, o_b = outs
        rows = lambda a, slot: a[slot * BLOCK:(slot + 1) * BLOCK]
        pairs = [jnp.where(low_o, rows(o_a, 0), rows(o_b, 0)), jnp.where(low_o, rows(o_a, 1), rows(o_b, 1)),
                 jnp.where(low_o, rows(o_b, 2), rows(o_a, 2)), jnp.where(low_o, rows(o_b, 3), rows(o_a, 3))]
        o_ref[qrows, :] = jnp.concatenate(pairs, axis=1).astype(BF16)


def _attention(qkv3, bias, sink, attn_width):
    bsz, seq, width = qkv3.shape
    kv_width = N_KV_HEADS * HEAD_DIM
    assert width == attn_width + 2 * kv_width and attn_width % (2 * kv_width) == 0
    assert seq % (Q_BLOCKS * BLOCK) == 0
    nb = seq // BLOCK
    rows = Q_BLOCKS * BLOCK
    kv_col = attn_width // (2 * kv_width)
    kv_spec = lambda off: pl.BlockSpec(
        (None, BLOCK, 2 * kv_width), lambda b, j: (b, jnp.clip(j * Q_BLOCKS + off, 0, nb - 1), kv_col))
    offsets = range(-1, Q_BLOCKS + 1)
    return pl.pallas_call(
        _attn_kernel,
        out_shape=jax.ShapeDtypeStruct((bsz, seq, attn_width), BF16),
        grid=(bsz, nb // Q_BLOCKS),
        in_specs=[pl.BlockSpec((None, rows, attn_width), lambda b, j: (b, j, 0))]
                 + [kv_spec(off) for off in offsets]
                 + [pl.BlockSpec(bias.shape, lambda b, j: (0, 0, 0, 0)), pl.BlockSpec(memory_space=pltpu.SMEM)],
        out_specs=pl.BlockSpec((None, rows, attn_width), lambda b, j: (b, j, 0)),
        compiler_params=_params("parallel", "parallel"),
        name="attention",
    )(*([qkv3] * (1 + len(offsets))), bias, sink)


def _ssm_operators(lam_re, lam_im, log_step, b_re, b_im, c_re, c_im, d):
    hp = lax.Precision.HIGH
    t_len = SSM_CHUNK
    lam_re, lam_im = lam_re.astype(F32), lam_im.astype(F32)
    dt = jnp.exp(log_step.astype(F32))[..., None]
    ld_re, ld_im = lam_re * dt, lam_im * dt
    mag = jnp.exp(ld_re)
    lb_re, lb_im = mag * jnp.cos(ld_im), mag * jnp.sin(ld_im)
    den = lam_re * lam_re + lam_im * lam_im
    nr, ni = lb_re - 1.0, lb_im
    coef_re = (nr * lam_re + ni * lam_im) / den
    coef_im = (ni * lam_re - nr * lam_im) / den
    b_re, b_im = b_re.astype(F32), b_im.astype(F32)
    bb_re = coef_re[..., None] * b_re - coef_im[..., None] * b_im
    bb_im = coef_re[..., None] * b_im + coef_im[..., None] * b_re
    g_cnt, n_cnt, p_cnt = bb_re.shape[1:]

    def powers(re, im, tau):
        m = jnp.exp(re * tau)
        return m * jnp.cos(im * tau), m * jnp.sin(im * tau)

    tau = jnp.arange(t_len + 1, dtype=F32)
    pw_re, pw_im = powers(ld_re[..., None], ld_im[..., None], tau)
    pwt_re, pwt_im = powers(ld_re[:, :, None], ld_im[:, :, None], tau[:, None])

    lanes = t_len * p_cnt
    ct_re = jnp.tile(c_re.astype(F32).transpose(0, 1, 3, 2), (1, 1, 1, t_len))
    ct_im = jnp.tile(c_im.astype(F32).transpose(0, 1, 3, 2), (1, 1, 1, t_len))

    def c_times(direction, pr, pi):
        pr, pi = jnp.repeat(pr, p_cnt, axis=-1), jnp.repeat(pi, p_cnt, axis=-1)
        cr, ci = ct_re[direction], ct_im[direction]
        return cr * pr - ci * pi, cr * pi + ci * pr

    k_lag = []
    for direction in range(2):
        cp_re, cp_im = c_times(direction, pw_re[direction][..., :t_len], pw_im[direction][..., :t_len])
        k_lag.append(jnp.einsum('gnq,gnm->gqm',
                                jnp.concatenate([bb_re[direction], -bb_im[direction]], axis=1),
                                jnp.concatenate([cp_re, cp_im], axis=1), precision=hp))
    skip = d.astype(F32).reshape(g_cnt, 1, p_cnt) * jnp.eye(p_cnt, dtype=F32)
    k_fwd = k_lag[0].at[:, :, :p_cnt].add(skip)
    k_bwd_rev = k_lag[1].reshape(g_cnt, p_cnt, t_len, p_cnt)[:, :, ::-1].reshape(g_cnt, p_cnt, lanes)
    zeros = jnp.zeros_like(k_fwd)
    f_pad = jnp.concatenate([zeros, k_fwd], axis=2)
    b_pad = jnp.concatenate([k_bwd_rev, zeros], axis=2)
    kbig = jnp.stack([f_pad[:, :, (t_len - s) * p_cnt:(2 * t_len - s) * p_cnt]
                      + b_pad[:, :, (t_len - 1 - s) * p_cnt:(2 * t_len - 1 - s) * p_cnt]
                      for s in range(t_len)], axis=1)
    kbig = kbig.reshape(g_cnt, lanes, lanes)

    def lift_b(direction, pr, pi):
        br = jnp.tile(bb_re[direction].transpose(0, 2, 1), (1, t_len, 1))
        bi = jnp.tile(bb_im[direction].transpose(0, 2, 1), (1, t_len, 1))
        pr, pi = jnp.repeat(pr, p_cnt, axis=1), jnp.repeat(pi, p_cnt, axis=1)
        return pr * br - pi * bi, pr * bi + pi * br

    bf_re, bf_im = lift_b(0, pwt_re[0][:, :t_len][:, ::-1], pwt_im[0][:, :t_len][:, ::-1])
    bw_re, bw_im = lift_b(1, pwt_re[1][:, :t_len], pwt_im[1][:, :t_len])
    bbig = jnp.concatenate([bf_re, bw_re, bf_im, bw_im], axis=-1)

    cf_re, cf_im = c_times(0, pw_re[0][..., 1:], pw_im[0][..., 1:])
    cb_re, cb_im = c_times(1, pw_re[1][..., :0:-1], pw_im[1][..., :0:-1])
    cbig = jnp.concatenate([cf_re, cb_re, -cf_im, -cb_im], axis=1)

    a_re = jnp.concatenate([pw_re[0][..., t_len], pw_re[1][..., t_len]], axis=-1)
    a_im = jnp.concatenate([pw_im[0][..., t_len], pw_im[1][..., t_len]], axis=-1)
    a16 = jnp.stack([a_re, a_im], axis=1)
    return bbig.astype(BF16), kbig.astype(BF16), cbig.astype(BF16), a16


def _ssm_kernel(u_ref, bbig_ref, kbig_ref, cbig_ref, a_ref, z_ref, v_scr, s_scr, *, bsz, rows_per_dot):
    rows = u_ref.shape[1]
    nc = rows // bsz
    half = 2 * SSM_STATE
    blocks = [slice(r, r + rows_per_dot) for r in range(0, rows, rows_per_dot)]

    for blk in blocks:
        v_scr[blk, :] = jnp.dot(u_ref[0, blk, :], bbig_ref[0], preferred_element_type=F32)

    a_re = a_ref[0, 0:1, :]
    a_im = a_ref[0, 1:2, :]
    is_fwd = lax.broadcasted_iota(jnp.int32, (bsz, half), 1) < SSM_STATE

    def scan_step(i, carry):
        s_re, s_im = carry
        rf = pl.multiple_of(i * bsz, bsz)
        rb = pl.multiple_of((nc - 1 - i) * bsz, bsz)
        s_scr[pl.ds(rf, bsz), 0:half] = s_re
        s_scr[pl.ds(rf, bsz), half:2 * half] = s_im
        v_re = jnp.where(is_fwd, v_scr[pl.ds(rf, bsz), 0:half], v_scr[pl.ds(rb, bsz), 0:half])
        v_im = jnp.where(is_fwd, v_scr[pl.ds(rf, bsz), half:2 * half], v_scr[pl.ds(rb, bsz), half:2 * half])
        n_re = a_re * s_re - a_im * s_im + v_re
        n_im = a_re * s_im + a_im * s_re + v_im
        return n_re, n_im

    zero = jnp.zeros((bsz, half), F32)
    lax.fori_loop(0, nc, scan_step, (zero, zero), unroll=2)

    is_fwd2 = lax.broadcasted_iota(jnp.int32, (bsz, 2 * half), 1) % half < SSM_STATE
    chunk = lambda c: s_scr[c * bsz:(c + 1) * bsz, :]
    for blk in blocks:
        entering = jnp.concatenate([jnp.where(is_fwd2, chunk(c), chunk(nc - 1 - c))
                                    for c in range(blk.start // bsz, blk.stop // bsz)], axis=0)
        y = (jnp.dot(u_ref[0, blk, :], kbig_ref[0], preferred_element_type=F32)
             + jnp.dot(entering.astype(BF16), cbig_ref[0], preferred_element_type=F32))
        z_ref[0, blk, :] = _gelu(y).astype(BF16)


def _ssm(u_t, bbig, kbig, cbig, a16, bsz):
    g_cnt, rows, width = u_t.shape
    rows_per_dot = min(rows, 512)
    n_state = bbig.shape[-1]
    return pl.pallas_call(
        functools.partial(_ssm_kernel, bsz=bsz, rows_per_dot=rows_per_dot),
        out_shape=jax.ShapeDtypeStruct((g_cnt, rows, width), BF16),
        grid=(g_cnt,),
        in_specs=[pl.BlockSpec((1, rows, width), lambda g: (g, 0, 0)),
                  pl.BlockSpec((1, width, n_state), lambda g: (g, 0, 0)),
                  pl.BlockSpec((1, width, width), lambda g: (g, 0, 0)),
                  pl.BlockSpec((1, n_state, width), lambda g: (g, 0, 0)),
                  pl.BlockSpec((1, 2, n_state // 2), lambda g: (g, 0, 0))],
        out_specs=pl.BlockSpec((1, rows, width), lambda g: (g, 0, 0)),
        scratch_shapes=[pltpu.VMEM((rows, n_state), F32),
                        pltpu.VMEM((rows, n_state), F32)],
        compiler_params=_params("parallel"),
        name="s5_mixer",
    )(u_t, bbig, kbig, cbig, a16)


def _merge_kernel(zt_ref, att_ref, x_ref, wglu_ref, bglu_ref, wout_ref, gpost_ref, o_ref,
                  z_scr, *, bsz, batch_major):
    for t1 in range(SSM_CHUNK // GROUPS_PER_VREG):
        for g1 in range(z_scr.shape[1] // LANES):
            vs = [zt_ref[g1 * GROUPS_PER_VREG + g2, :, t1 * LANES:(t1 + 1) * LANES].astype(F32)
                  for g2 in range(GROUPS_PER_VREG)]
            for t2, tile in enumerate(_atom_transpose(vs)):
                t = t1 * GROUPS_PER_VREG + t2
                z_scr[t * bsz:(t + 1) * bsz, g1 * LANES:(g1 + 1) * LANES] = tile
    z = z_scr[...]
    gate = jnp.dot(z.astype(BF16), wglu_ref[...], preferred_element_type=F32) + bglu_ref[...]
    y_ssm = z * jax.nn.sigmoid(gate)
    y_att = att_ref[...].astype(F32)
    merged = jnp.concatenate([_rms(y_ssm).astype(BF16), _rms(y_att).astype(BF16)], axis=-1)
    y = jnp.dot(merged, wout_ref[...], preferred_element_type=F32)
    o_ref[...] = _load_stream(x_ref, o_ref.shape[1], batch_major) + _rms(y, gpost_ref[...])


def _merge(z_t, att2d, x2d, w_glu, b_glu, w_out, g_post, bsz, batch_major):
    d = w_out.shape[1]
    t = x2d.size // d
    g_cnt, nc, _, lanes = z_t.shape
    ws, wa = g_cnt * SSM_GROUP, att2d.shape[1]
    tm = SSM_CHUNK * bsz
    row = lambda w: pl.BlockSpec((tm, w), lambda i: (i, 0))
    full = lambda a: pl.BlockSpec(a.shape, lambda i: (0,) * a.ndim)
    return pl.pallas_call(
        functools.partial(_merge_kernel, bsz=bsz, batch_major=batch_major),
        out_shape=jax.ShapeDtypeStruct((t, d), F32),
        grid=(nc,),
        in_specs=[pl.BlockSpec((g_cnt, None, bsz, lanes), lambda i: (0, i, 0, 0)),
                  row(wa), _stream_spec(tm, d, bsz, batch_major), full(w_glu), full(b_glu),
                  full(w_out), full(g_post)],
        out_specs=row(d),
        scratch_shapes=[pltpu.VMEM((tm, ws), F32)],
        compiler_params=_params("parallel"),
        name="merge",
    )(z_t, att2d, x2d, w_glu, b_glu, w_out, g_post)


def _ffn_kernel(xp_ref, x_ref, xn_ref, wup_ref, cw_ref, cb_ref, wdown_ref, gpost_ref, o_ref,
                h_scr, a_scr, act_scr, *, d_ff, ff_chunk, halo, out_batch_major):
    i = pl.program_id(0)
    n_tiles = pl.num_programs(0)
    tm = x_ref.shape[0]
    keep_prev = jnp.where(i == 0, 0.0, 1.0)
    keep_next = jnp.where(i == n_tiles - 1, 0.0, 1.0)
    h_scr[0:halo, :] = (_rms(xp_ref[...]) * keep_prev).astype(BF16)
    h_scr[halo:halo + tm, :] = _rms(x_ref[...]).astype(BF16)
    h_scr[halo + tm:, :] = (_rms(xn_ref[...]) * keep_next).astype(BF16)
    n_chunks = d_ff // ff_chunk

    def cols(base, c):
        return pl.ds(base + c * ff_chunk, ff_chunk)

    def up(c, slot):
        h = h_scr[...]
        for part, base in enumerate((0, d_ff)):
            a_scr[slot, part] = jnp.dot(h, wup_ref[:, cols(base, c)], preferred_element_type=F32)

    def post(c, slot):
        def conv(part, base):
            w = cw_ref[:, cols(base, c)]
            return (a_scr[slot, part, 0:tm] * w[0:1] + a_scr[slot, part, halo:halo + tm] * w[1:2]
                    + a_scr[slot, part, 2 * halo:2 * halo + tm] * w[2:3] + cb_ref[:, cols(base, c)])

        act = _gelu(conv(1, d_ff)) * conv(0, 0)
        act_scr[:, cols(0, c)] = act.astype(BF16)

    n_slots = a_scr.shape[0]
    up(0, 0)
    for c in range(n_chunks - 1):
        up(c + 1, (c + 1) % n_slots)
        post(c, c % n_slots)
    k_head = (n_chunks - 1) * ff_chunk
    y = jnp.dot(act_scr[:, :k_head], wdown_ref[:k_head, :], preferred_element_type=F32)
    post(n_chunks - 1, (n_chunks - 1) % n_slots)
    y = y + jnp.dot(act_scr[:, k_head:], wdown_ref[k_head:, :], preferred_element_type=F32)
    out = x_ref[...] + _rms(y, gpost_ref[...])
    if out_batch_major:
        d = out.shape[1]
        for t in range(tm // halo):
            o_ref[:, t * d:(t + 1) * d] = out[t * halo:(t + 1) * halo]
    else:
        o_ref[...] = out


def _ffn(x2d, w_up, conv_w, conv_b, w_down, g_post, bsz, tm, ff_chunk, out_batch_major):
    t, d = x2d.shape
    d_ff = w_down.shape[0]
    halo = bsz
    hb = tm // halo
    n_halo_blocks = t // halo
    full = lambda a: pl.BlockSpec(a.shape, lambda i: (0,) * a.ndim)
    return pl.pallas_call(
        functools.partial(_ffn_kernel, d_ff=d_ff, ff_chunk=ff_chunk, halo=halo, out_batch_major=out_batch_major),
        out_shape=jax.ShapeDtypeStruct((bsz, (t // bsz) * d) if out_batch_major else (t, d), F32),
        grid=(t // tm,),
        in_specs=[pl.BlockSpec((halo, d), lambda i: (jnp.maximum(i * hb - 1, 0), 0)),
                  pl.BlockSpec((tm, d), lambda i: (i, 0)),
                  pl.BlockSpec((halo, d), lambda i: (jnp.minimum((i + 1) * hb, n_halo_blocks - 1), 0)),
                  full(w_up), full(conv_w), full(conv_b), full(w_down), full(g_post)],
        out_specs=_stream_spec(tm, d, bsz, out_batch_major),
        scratch_shapes=[pltpu.VMEM((tm + 2 * halo, d), BF16),
                        pltpu.VMEM((3, 2, tm + 2 * halo, ff_chunk), F32),
                        pltpu.VMEM((tm, d_ff), BF16)],
        compiler_params=_params("parallel"),
        name="conv_ffn",
    )(x2d, x2d, x2d, w_up, conv_w, conv_b, w_down, g_post)


def _tile_sizes(bsz, seq):
    steps = 32 if seq % 32 == 0 else 1
    return steps * bsz, 256


def kernel(x, rel_bias, pre_mix_norm, w_in, lam_re, lam_im, log_step, b_re, b_im, c_re, c_im, ssm_d, w_glu,
           b_glu, attn_sink, ssm_out_norm, attn_out_norm, w_out, post_mix_norm, pre_ffn_norm, w_up, conv_w,
           conv_b, w_down, post_ffn_norm):
    bsz, seq, d_model = x.shape
    depth = w_in.shape[0]
    ssm_width = ssm_d.shape[1]
    attn_width = w_out.shape[1] - ssm_width
    g_cnt = ssm_width // SSM_GROUP
    nc = seq // SSM_CHUNK
    assert seq % BLOCK == 0 and bsz % 16 == 0 and ssm_width % LANES == 0
    assert attn_width == 2 * len(HEAD_ORDER[0]) * HEAD_DIM and rel_bias.shape == (N_BUCKETS, attn_width // HEAD_DIM)
    tm_ffn, ff_chunk = _tile_sizes(bsz, seq)
    assert w_down.shape[1] % ff_chunk == 0

    bias = _bias_table(rel_bias)
    ssm_ops = jax.vmap(_ssm_operators)(lam_re, lam_im, log_step, b_re, b_im, c_re, c_im, ssm_d)
    row = lambda a: a.astype(F32).reshape(1, -1)
    q_scale = jnp.concatenate([jnp.ones((ssm_width,), F32), jnp.full((attn_width,), HEAD_DIM ** -0.5, F32),
                               jnp.ones((w_in.shape[2] - ssm_width - attn_width,), F32)])

    x2d = x.reshape(bsz, seq * d_model)
    for l in range(depth):
        first, last = l == 0, l == depth - 1
        rows_scaled = lambda w, g: (g.astype(F32)[:, None] * w).astype(BF16)
        u_t, qkv = _proj(x2d, rows_scaled(w_in[l] * q_scale, pre_mix_norm[l]), bsz, ssm_width, first)
        qkv3 = qkv.reshape(seq, bsz, -1).transpose(1, 0, 2)
        att = _attention(qkv3, bias, attn_sink[l].astype(F32), attn_width)
        att2d = att.transpose(1, 0, 2).reshape(seq * bsz, attn_width)

        z_t = _ssm(u_t.reshape(g_cnt, nc * bsz, -1), *(op[l] for op in ssm_ops), bsz).reshape(u_t.shape)

        branch_gain = jnp.concatenate([ssm_out_norm[l], attn_out_norm[l]])
        x2d = _merge(z_t, att2d, x2d, w_glu[l].astype(BF16), row(b_glu[l]), rows_scaled(w_out[l], branch_gain),
                     row(post_mix_norm[l]), bsz, first)
        x2d = _ffn(x2d, rows_scaled(w_up[l], pre_ffn_norm[l]), conv_w[l].astype(F32), row(conv_b[l]),
                   w_down[l].astype(BF16), row(post_ffn_norm[l]), bsz, tm_ffn, ff_chunk, last)
    return x2d.reshape(bsz, seq, d_model)
```

```python
import functools
import math

import numpy as np
import jax
import jax.numpy as jnp
from jax import lax
from jax.experimental import pallas as pl
from jax.experimental.pallas import tpu as pltpu

F32 = jnp.float32
BF16 = jnp.bfloat16

LANES = 128
SSM_GROUP = 16
SSM_STATE = 64
HEAD_DIM = 64
N_KV_HEADS = 2
WINDOW = 128
BLOCK = 128
N_BUCKETS = 32
MAX_DISTANCE = 128
EPS = 1e-6
NEG_INF = -1e30
SSM_CHUNK = 16
Q_BLOCKS = 1
GROUPS_PER_VREG = LANES // SSM_GROUP
CHUNKS_PER_STEP = 2

VMEM_LIMIT_BYTES = 56 * 1024 * 1024

HEAD_ORDER = ((0, 2, 5, 7), (1, 3, 4, 6))


def _params(*sem):
    return pltpu.CompilerParams(dimension_semantics=sem, vmem_limit_bytes=VMEM_LIMIT_BYTES)


def _rms(x, g):
    return x * lax.rsqrt(jnp.mean(x * x, axis=-1, keepdims=True) + EPS) * g


def _gelu(x):
    c = math.sqrt(2.0 / math.pi)
    return x * (0.5 + 0.5 * jnp.tanh(x * (c + (c * 0.044715) * (x * x))))


def _atom_transpose(vs):
    atom = lax.broadcasted_iota(jnp.int32, vs[0].shape, 1) // SSM_GROUP
    for d in (4, 2, 1):
        keep = (atom & d) == 0
        nxt = list(vs)
        for i in range(GROUPS_PER_VREG):
            if i & d == 0:
                a, b = vs[i], vs[i + d]
                nxt[i] = jnp.where(keep, a, pltpu.roll(b, SSM_GROUP * d, axis=1))
                nxt[i + d] = jnp.where(keep, pltpu.roll(a, LANES - SSM_GROUP * d, axis=1), b)
        vs = nxt
    return vs


def _bucket_table():
    half = N_BUCKETS // 2
    max_exact = half // 2
    qi = np.arange(BLOCK)[:, None]
    sj = np.arange(3 * BLOCK)[None, :]
    rel = sj - BLOCK - qi
    n = np.abs(rel)
    nf = np.maximum(n, 1).astype(np.float64)
    large = max_exact + (np.log(nf / max_exact) / math.log(MAX_DISTANCE / max_exact)
                         * (half - max_exact)).astype(np.int32)
    large = np.minimum(large, half - 1)
    bucket = np.where(rel > 0, half, 0) + np.where(n < max_exact, n, large)
    return np.where(n <= WINDOW, bucket, -1).astype(np.int32)


def _bias_kernel(bucket_ref, rel_bias_ref, out_ref):
    bucket = bucket_ref[...]
    col = lax.broadcasted_iota(jnp.int32, bucket.shape, 1)
    for grp, heads in enumerate(HEAD_ORDER):
        for slot, h in enumerate(heads):
            acc = jnp.full(bucket.shape, NEG_INF, F32)
            for b in range(N_BUCKETS):
                acc = jnp.where(bucket == b, rel_bias_ref[b, h], acc)
            for variant in range(4):
                lo = BLOCK if variant & 1 else 0
                hi = 2 * BLOCK if variant & 2 else 3 * BLOCK
                out_ref[variant, grp, slot * BLOCK:(slot + 1) * BLOCK, :] = jnp.where(
                    (col >= lo) & (col < hi), acc, NEG_INF)


def _bias_table(rel_bias):
    rows = len(HEAD_ORDER[0]) * BLOCK
    bucket = jnp.asarray(_bucket_table())
    return pl.pallas_call(
        _bias_kernel,
        out_shape=jax.ShapeDtypeStruct((4, len(HEAD_ORDER), rows, 3 * BLOCK), F32),
        in_specs=[pl.BlockSpec(memory_space=pltpu.VMEM), pl.BlockSpec(memory_space=pltpu.SMEM)],
        out_specs=pl.BlockSpec(memory_space=pltpu.VMEM),
        name="bias_table",
    )(bucket, rel_bias.astype(F32))


def _stream_spec(tm, d, bsz, batch_major):
    if batch_major:
        return pl.BlockSpec((bsz, (tm // bsz) * d), lambda i: (0, i))
    return pl.BlockSpec((tm, d), lambda i: (i, 0))


def _load_stream(x_ref, d, batch_major):
    if not batch_major:
        return x_ref[...]
    return jnp.concatenate([x_ref[:, t * d:(t + 1) * d] for t in range(x_ref.shape[1] // d)], axis=0)


def _proj_kernel(x_ref, g_ref, w_ref, ut_ref, qkv_ref, *, bsz, ssm_width, batch_major):
    h = _rms(_load_stream(x_ref, g_ref.shape[1], batch_major), g_ref[...]).astype(BF16)
    r = jnp.dot(h, w_ref[...], preferred_element_type=F32)
    qkv_ref[...] = r[:, ssm_width:].astype(BF16)
    for ck in range(ut_ref.shape[1]):
        for t1 in range(SSM_CHUNK // GROUPS_PER_VREG):
            step0 = ck * SSM_CHUNK + t1 * GROUPS_PER_VREG
            for g1 in range(ssm_width // LANES):
                vs = [r[(step0 + t2) * bsz:(step0 + t2 + 1) * bsz, g1 * LANES:(g1 + 1) * LANES]
                      for t2 in range(GROUPS_PER_VREG)]
                for g2, tile in enumerate(_atom_transpose(vs)):
                    ut_ref[g1 * GROUPS_PER_VREG + g2, ck, :, t1 * LANES:(t1 + 1) * LANES] = tile.astype(BF16)


def _proj(x2d, g, w, bsz, ssm_width, batch_major):
    d, n = w.shape
    t = x2d.size // d
    tm = CHUNKS_PER_STEP * SSM_CHUNK * bsz
    nc = t // (SSM_CHUNK * bsz)
    g_cnt = ssm_width // SSM_GROUP
    lanes = SSM_CHUNK * SSM_GROUP
    return pl.pallas_call(
        functools.partial(_proj_kernel, bsz=bsz, ssm_width=ssm_width, batch_major=batch_major),
        out_shape=(jax.ShapeDtypeStruct((g_cnt, nc, bsz, lanes), BF16),
                   jax.ShapeDtypeStruct((t, n - ssm_width), BF16)),
        grid=(t // tm,),
        in_specs=[_stream_spec(tm, d, bsz, batch_major),
                  pl.BlockSpec((1, d), lambda i: (0, 0)),
                  pl.BlockSpec((d, n), lambda i: (0, 0))],
        out_specs=(pl.BlockSpec((g_cnt, CHUNKS_PER_STEP, bsz, lanes), lambda i: (0, i, 0, 0)),
                   pl.BlockSpec((tm, n - ssm_width), lambda i: (i, 0))),
        compiler_params=_params("parallel"),
        name="proj",
    )(x2d, g, w)


def _attn_kernel(q_ref, *refs):
    n_kb = Q_BLOCKS + 2
    kv_refs = refs[:n_kb]
    bias_ref, sink_ref, o_ref = refs[n_kb:]
    j = pl.program_id(1)
    nj = pl.num_programs(1)
    kv_width = N_KV_HEADS * HEAD_DIM
    k = jnp.concatenate([r[:, :kv_width] for r in kv_refs], axis=0)
    v = jnp.concatenate([r[:, kv_width:] for r in kv_refs], axis=0)
    swap = lambda a: jnp.concatenate([a[:, HEAD_DIM:], a[:, :HEAD_DIM]], axis=1)
    k_sw, v_sw = swap(k), swap(v)
    low = lax.broadcasted_iota(jnp.int32, k.shape, 1) < HEAD_DIM
    zero = jnp.zeros_like(k)
    k_even = (jnp.where(low, k, zero), jnp.where(low, k_sw, zero))
    k_odd = (jnp.where(low, zero, k_sw), jnp.where(low, zero, k))
    ones = jnp.ones_like(v)
    v_ext = (jnp.concatenate([v, ones], axis=1), jnp.concatenate([v_sw, ones], axis=1))
    qk = lambda a, b: lax.dot_general(a, b, (((1,), (1,)), ((), ())), preferred_element_type=F32)
    low_o = lax.broadcasted_iota(jnp.int32, (BLOCK, LANES), 1) < HEAD_DIM
    for sub in range(Q_BLOCKS):
        variant = jnp.where(j == 0, 1, 0) if sub == 0 else 0
        variant = variant + (jnp.where(j == nj - 1, 2, 0) if sub == Q_BLOCKS - 1 else 0)
        qrows = slice(sub * BLOCK, (sub + 1) * BLOCK)
        keys = lambda a: a[sub * BLOCK:(sub + 3) * BLOCK]
        q_kv = (jnp.concatenate([q_ref[qrows, 0:LANES], q_ref[qrows, LANES:2 * LANES]], axis=0),
                jnp.concatenate([q_ref[qrows, 2 * LANES:3 * LANES], q_ref[qrows, 3 * LANES:4 * LANES]], axis=0))
        scores = (jnp.concatenate([qk(q_kv[0], keys(k_even[0])), qk(q_kv[1], keys(k_odd[1]))], axis=0),
                  jnp.concatenate([qk(q_kv[0], keys(k_odd[0])), qk(q_kv[1], keys(k_even[1]))], axis=0))
        outs = []
        for grp, heads in enumerate(HEAD_ORDER):
            sink = jnp.concatenate([jnp.full((BLOCK, LANES), sink_ref[h], F32) for h in heads], axis=0)
            s = scores[grp] + bias_ref[variant, grp]
            m = jnp.maximum(jnp.broadcast_to(jnp.max(s, axis=-1, keepdims=True), sink.shape), sink)
            e = jnp.concatenate([jnp.exp(s[:, c * LANES:(c + 1) * LANES] - m)
                                 for c in range(s.shape[1] // LANES)], axis=1).astype(BF16)
            o = jnp.dot(e, keys(v_ext[grp]), preferred_element_type=F32)
            outs.append(o[:, :LANES] / (o[:, LANES:] + jnp.exp(sink - m)))
        o_a, o_b = outs
        rows = lambda a, slot: a[slot * BLOCK:(slot + 1) * BLOCK]
        pairs = [jnp.where(low_o, rows(o_a, 0), rows(o_b, 0)), jnp.where(low_o, rows(o_a, 1), rows(o_b, 1)),
                 jnp.where(low_o, rows(o_b, 2), rows(o_a, 2)), jnp.where(low_o, rows(o_b, 3), rows(o_a, 3))]
        o_ref[qrows, :] = jnp.concatenate(pairs, axis=1).astype(BF16)


def _attention(qkv3, bias, sink, attn_width):
    bsz, seq, width = qkv3.shape
    kv_width = N_KV_HEADS * HEAD_DIM
    assert width == attn_width + 2 * kv_width and attn_width % (2 * kv_width) == 0
    assert seq % (Q_BLOCKS * BLOCK) == 0
    nb = seq // BLOCK
    rows = Q_BLOCKS * BLOCK
    kv_col = attn_width // (2 * kv_width)
    kv_spec = lambda off: pl.BlockSpec(
        (None, BLOCK, 2 * kv_width), lambda b, j: (b, jnp.clip(j * Q_BLOCKS + off, 0, nb - 1), kv_col))
    offsets = range(-1, Q_BLOCKS + 1)
    return pl.pallas_call(
        _attn_kernel,
        out_shape=jax.ShapeDtypeStruct((bsz, seq, attn_width), BF16),
        grid=(bsz, nb // Q_BLOCKS),
        in_specs=[pl.BlockSpec((None, rows, attn_width), lambda b, j: (b, j, 0))]
                 + [kv_spec(off) for off in offsets]
                 + [pl.BlockSpec(bias.shape, lambda b, j: (0, 0, 0, 0)), pl.BlockSpec(memory_space=pltpu.SMEM)],
        out_specs=pl.BlockSpec((None, rows, attn_width), lambda b, j: (b, j, 0)),
        compiler_params=_params("parallel", "parallel"),
        name="attention",
    )(*([qkv3] * (1 + len(offsets))), bias, sink)


def _ssm_operators(lam_re, lam_im, log_step, b_re, b_im, c_re, c_im, d):
    hp = lax.Precision.HIGH
    t_len = SSM_CHUNK
    lam_re, lam_im = lam_re.astype(F32), lam_im.astype(F32)
    dt = jnp.exp(log_step.astype(F32))[..., None]
    ld_re, ld_im = lam_re * dt, lam_im * dt
    mag = jnp.exp(ld_re)
    lb_re, lb_im = mag * jnp.cos(ld_im), mag * jnp.sin(ld_im)
    den = lam_re * lam_re + lam_im * lam_im
    nr, ni = lb_re - 1.0, lb_im
    coef_re = (nr * lam_re + ni * lam_im) / den
    coef_im = (ni * lam_re - nr * lam_im) / den
    b_re, b_im = b_re.astype(F32), b_im.astype(F32)
    bb_re = coef_re[..., None] * b_re - coef_im[..., None] * b_im
    bb_im = coef_re[..., None] * b_im + coef_im[..., None] * b_re
    g_cnt, n_cnt, p_cnt = bb_re.shape[1:]

    def powers(re, im, tau):
        m = jnp.exp(re * tau)
        return m * jnp.cos(im * tau), m * jnp.sin(im * tau)

    tau = jnp.arange(t_len + 1, dtype=F32)
    pw_re, pw_im = powers(ld_re[..., None], ld_im[..., None], tau)
    pwt_re, pwt_im = powers(ld_re[:, :, None], ld_im[:, :, None], tau[:, None])

    lanes = t_len * p_cnt
    ct_re = jnp.tile(c_re.astype(F32).transpose(0, 1, 3, 2), (1, 1, 1, t_len))
    ct_im = jnp.tile(c_im.astype(F32).transpose(0, 1, 3, 2), (1, 1, 1, t_len))

    def c_times(direction, pr, pi):
        pr, pi = jnp.repeat(pr, p_cnt, axis=-1), jnp.repeat(pi, p_cnt, axis=-1)
        cr, ci = ct_re[direction], ct_im[direction]
        return cr * pr - ci * pi, cr * pi + ci * pr

    k_lag = []
    for direction in range(2):
        cp_re, cp_im = c_times(direction, pw_re[direction][..., :t_len], pw_im[direction][..., :t_len])
        k_lag.append(jnp.einsum('gnq,gnm->gqm',
                                jnp.concatenate([bb_re[direction], -bb_im[direction]], axis=1),
                                jnp.concatenate([cp_re, cp_im], axis=1), precision=hp))
    skip = d.astype(F32).reshape(g_cnt, 1, p_cnt) * jnp.eye(p_cnt, dtype=F32)
    k_fwd = k_lag[0].at[:, :, :p_cnt].add(skip)
    k_bwd_rev = k_lag[1].reshape(g_cnt, p_cnt, t_len, p_cnt)[:, :, ::-1].reshape(g_cnt, p_cnt, lanes)
    zeros = jnp.zeros_like(k_fwd)
    f_pad = jnp.concatenate([zeros, k_fwd], axis=2)
    b_pad = jnp.concatenate([k_bwd_rev, zeros], axis=2)
    kbig = jnp.stack([f_pad[:, :, (t_len - s) * p_cnt:(2 * t_len - s) * p_cnt]
                      + b_pad[:, :, (t_len - 1 - s) * p_cnt:(2 * t_len - 1 - s) * p_cnt]
                      for s in range(t_len)], axis=1)
    kbig = kbig.reshape(g_cnt, lanes, lanes)

    def lift_b(direction, pr, pi):
        br = jnp.tile(bb_re[direction].transpose(0, 2, 1), (1, t_len, 1))
        bi = jnp.tile(bb_im[direction].transpose(0, 2, 1), (1, t_len, 1))
        pr, pi = jnp.repeat(pr, p_cnt, axis=1), jnp.repeat(pi, p_cnt, axis=1)
        return pr * br - pi * bi, pr * bi + pi * br

    bf_re, bf_im = lift_b(0, pwt_re[0][:, :t_len][:, ::-1], pwt_im[0][:, :t_len][:, ::-1])
    bw_re, bw_im = lift_b(1, pwt_re[1][:, :t_len], pwt_im[1][:, :t_len])
    bbig = jnp.concatenate([bf_re, bw_re, bf_im, bw_im], axis=-1)

    cf_re, cf_im = c_times(0, pw_re[0][..., 1:], pw_im[0][..., 1:])
    cb_re, cb_im = c_times(1, pw_re[1][..., :0:-1], pw_im[1][..., :0:-1])
    cbig = jnp.concatenate([cf_re, cb_re, -cf_im, -cb_im], axis=1)

    a_re = jnp.concatenate([pw_re[0][..., t_len], pw_re[1][..., t_len]], axis=-1)
    a_im = jnp.concatenate([pw_im[0][..., t_len], pw_im[1][..., t_len]], axis=-1)
    a16 = jnp.stack([a_re, a_im], axis=1)
    return bbig.astype(BF16), kbig.astype(BF16), cbig.astype(BF16), a16


def _ssm_kernel(u_ref, bbig_ref, kbig_ref, cbig_ref, a_ref, z_ref, v_scr, s_scr, *, bsz, rows_per_dot):
    rows = u_ref.shape[1]
    nc = rows // bsz
    half = 2 * SSM_STATE
    blocks = [slice(r, r + rows_per_dot) for r in range(0, rows, rows_per_dot)]

    for blk in blocks:
        v_scr[blk, :] = jnp.dot(u_ref[0, blk, :], bbig_ref[0], preferred_element_type=F32)

    a_re = a_ref[0, 0:1, :]
    a_im = a_ref[0, 1:2, :]
    is_fwd = lax.broadcasted_iota(jnp.int32, (bsz, half), 1) < SSM_STATE

    def scan_step(i, carry):
        s_re, s_im = carry
        rf = pl.multiple_of(i * bsz, bsz)
        rb = pl.multiple_of((nc - 1 - i) * bsz, bsz)
        s_scr[pl.ds(rf, bsz), 0:half] = s_re
        s_scr[pl.ds(rf, bsz), half:2 * half] = s_im
        v_re = jnp.where(is_fwd, v_scr[pl.ds(rf, bsz), 0:half], v_scr[pl.ds(rb, bsz), 0:half])
        v_im = jnp.where(is_fwd, v_scr[pl.ds(rf, bsz), half:2 * half], v_scr[pl.ds(rb, bsz), half:2 * half])
        n_re = a_re * s_re - a_im * s_im + v_re
        n_im = a_re * s_im + a_im * s_re + v_im
        return n_re, n_im

    zero = jnp.zeros((bsz, half), F32)
    lax.fori_loop(0, nc, scan_step, (zero, zero), unroll=2)

    is_fwd2 = lax.broadcasted_iota(jnp.int32, (bsz, 2 * half), 1) % half < SSM_STATE
    chunk = lambda c: s_scr[c * bsz:(c + 1) * bsz, :]
    for blk in blocks:
        entering = jnp.concatenate([jnp.where(is_fwd2, chunk(c), chunk(nc - 1 - c))
                                    for c in range(blk.start // bsz, blk.stop // bsz)], axis=0)
        y = (jnp.dot(u_ref[0, blk, :], kbig_ref[0], preferred_element_type=F32)
             + jnp.dot(entering.astype(BF16), cbig_ref[0], preferred_element_type=F32))
        z_ref[0, blk, :] = _gelu(y).astype(BF16)


def _ssm(u_t, bbig, kbig, cbig, a16, bsz):
    g_cnt, rows, width = u_t.shape
    rows_per_dot = min(rows, 512)
    n_state = bbig.shape[-1]
    return pl.pallas_call(
        functools.partial(_ssm_kernel, bsz=bsz, rows_per_dot=rows_per_dot),
        out_shape=jax.ShapeDtypeStruct((g_cnt, rows, width), BF16),
        grid=(g_cnt,),
        in_specs=[pl.BlockSpec((1, rows, width), lambda g: (g, 0, 0)),
                  pl.BlockSpec((1, width, n_state), lambda g: (g, 0, 0)),
                  pl.BlockSpec((1, width, width), lambda g: (g, 0, 0)),
                  pl.BlockSpec((1, n_state, width), lambda g: (g, 0, 0)),
                  pl.BlockSpec((1, 2, n_state // 2), lambda g: (g, 0, 0))],
        out_specs=pl.BlockSpec((1, rows, width), lambda g: (g, 0, 0)),
        scratch_shapes=[pltpu.VMEM((rows, n_state), F32),
                        pltpu.VMEM((rows, n_state), F32)],
        compiler_params=_params("parallel"),
        name="s5_mixer",
    )(u_t, bbig, kbig, cbig, a16)


def _merge_kernel(zt_ref, att_ref, x_ref, wglu_ref, bglu_ref, gssm_ref, gatt_ref, wout_ref, gpost_ref, o_ref,
                  z_scr, *, bsz, batch_major):
    for ck in range(zt_ref.shape[1]):
        for t1 in range(SSM_CHUNK // GROUPS_PER_VREG):
            for g1 in range(z_scr.shape[1] // LANES):
                vs = [zt_ref[g1 * GROUPS_PER_VREG + g2, ck, :, t1 * LANES:(t1 + 1) * LANES].astype(F32)
                      for g2 in range(GROUPS_PER_VREG)]
                for t2, tile in enumerate(_atom_transpose(vs)):
                    t = ck * SSM_CHUNK + t1 * GROUPS_PER_VREG + t2
                    z_scr[t * bsz:(t + 1) * bsz, g1 * LANES:(g1 + 1) * LANES] = tile
    z = z_scr[...]
    gate = jnp.dot(z.astype(BF16), wglu_ref[...], preferred_element_type=F32) + bglu_ref[...]
    y_ssm = z * jax.nn.sigmoid(gate)
    y_att = att_ref[...].astype(F32)
    merged = jnp.concatenate([_rms(y_ssm, gssm_ref[...]).astype(BF16),
                              _rms(y_att, gatt_ref[...]).astype(BF16)], axis=-1)
    y = jnp.dot(merged, wout_ref[...], preferred_element_type=F32)
    o_ref[...] = _load_stream(x_ref, o_ref.shape[1], batch_major) + _rms(y, gpost_ref[...])


def _merge(z_t, att2d, x2d, w_glu, b_glu, g_ssm, g_att, w_out, g_post, bsz, batch_major):
    d = w_out.shape[1]
    t = x2d.size // d
    g_cnt, nc, _, lanes = z_t.shape
    ws, wa = g_cnt * SSM_GROUP, att2d.shape[1]
    tm = CHUNKS_PER_STEP * SSM_CHUNK * bsz
    row = lambda w: pl.BlockSpec((tm, w), lambda i: (i, 0))
    full = lambda a: pl.BlockSpec(a.shape, lambda i: (0,) * a.ndim)
    return pl.pallas_call(
        functools.partial(_merge_kernel, bsz=bsz, batch_major=batch_major),
        out_shape=jax.ShapeDtypeStruct((t, d), F32),
        grid=(t // tm,),
        in_specs=[pl.BlockSpec((g_cnt, CHUNKS_PER_STEP, bsz, lanes), lambda i: (0, i, 0, 0)),
                  row(wa), _stream_spec(tm, d, bsz, batch_major), full(w_glu), full(b_glu), full(g_ssm), full(g_att),
                  full(w_out), full(g_post)],
        out_specs=row(d),
        scratch_shapes=[pltpu.VMEM((tm, ws), F32)],
        compiler_params=_params("parallel"),
        name="merge",
    )(z_t, att2d, x2d, w_glu, b_glu, g_ssm, g_att, w_out, g_post)


def _ffn_kernel(xp_ref, x_ref, xn_ref, gpre_ref, wup_ref, cw_ref, cb_ref, wdown_ref, gpost_ref, o_ref,
                h_scr, a_scr, act_scr, *, d_ff, ff_chunk, halo, out_batch_major):
    i = pl.program_id(0)
    n_tiles = pl.num_programs(0)
    tm = x_ref.shape[0]
    keep_prev = jnp.where(i == 0, 0.0, 1.0)
    keep_next = jnp.where(i == n_tiles - 1, 0.0, 1.0)
    g = gpre_ref[...]
    h_scr[0:halo, :] = (_rms(xp_ref[...], g) * keep_prev).astype(BF16)
    h_scr[halo:halo + tm, :] = _rms(x_ref[...], g).astype(BF16)
    h_scr[halo + tm:, :] = (_rms(xn_ref[...], g) * keep_next).astype(BF16)
    n_chunks = d_ff // ff_chunk

    def cols(base, c):
        return pl.ds(base + c * ff_chunk, ff_chunk)

    def up(c, slot):
        h = h_scr[...]
        for part, base in enumerate((0, d_ff)):
            a_scr[slot, part] = jnp.dot(h, wup_ref[:, cols(base, c)], preferred_element_type=F32)

    def post(c, slot):
        def conv(part, base):
            w = cw_ref[:, cols(base, c)]
            return (a_scr[slot, part, 0:tm] * w[0:1] + a_scr[slot, part, halo:halo + tm] * w[1:2]
                    + a_scr[slot, part, 2 * halo:2 * halo + tm] * w[2:3] + cb_ref[:, cols(base, c)])

        act = _gelu(conv(1, d_ff)) * conv(0, 0)
        act_scr[:, cols(0, c)] = act.astype(BF16)

    n_slots = a_scr.shape[0]
    up(0, 0)
    for c in range(n_chunks - 1):
        up(c + 1, (c + 1) % n_slots)
        post(c, c % n_slots)
    k_head = (n_chunks - 1) * ff_chunk
    y = jnp.dot(act_scr[:, :k_head], wdown_ref[:k_head, :], preferred_element_type=F32)
    post(n_chunks - 1, (n_chunks - 1) % n_slots)
    y = y + jnp.dot(act_scr[:, k_head:], wdown_ref[k_head:, :], preferred_element_type=F32)
    out = x_ref[...] + _rms(y, gpost_ref[...])
    if out_batch_major:
        d = out.shape[1]
        for t in range(tm // halo):
            o_ref[:, t * d:(t + 1) * d] = out[t * halo:(t + 1) * halo]
    else:
        o_ref[...] = out


def _ffn(x2d, g_pre, w_up, conv_w, conv_b, w_down, g_post, bsz, tm, ff_chunk, out_batch_major):
    t, d = x2d.shape
    d_ff = w_down.shape[0]
    halo = bsz
    hb = tm // halo
    n_halo_blocks = t // halo
    full = lambda a: pl.BlockSpec(a.shape, lambda i: (0,) * a.ndim)
    return pl.pallas_call(
        functools.partial(_ffn_kernel, d_ff=d_ff, ff_chunk=ff_chunk, halo=halo, out_batch_major=out_batch_major),
        out_shape=jax.ShapeDtypeStruct((bsz, (t // bsz) * d) if out_batch_major else (t, d), F32),
        grid=(t // tm,),
        in_specs=[pl.BlockSpec((halo, d), lambda i: (jnp.maximum(i * hb - 1, 0), 0)),
                  pl.BlockSpec((tm, d), lambda i: (i, 0)),
                  pl.BlockSpec((halo, d), lambda i: (jnp.minimum((i + 1) * hb, n_halo_blocks - 1), 0)),
                  full(g_pre), full(w_up), full(conv_w), full(conv_b), full(w_down), full(g_post)],
        out_specs=_stream_spec(tm, d, bsz, out_batch_major),
        scratch_shapes=[pltpu.VMEM((tm + 2 * halo, d), BF16),
                        pltpu.VMEM((3, 2, tm + 2 * halo, ff_chunk), F32),
                        pltpu.VMEM((tm, d_ff), BF16)],
        compiler_params=_params("parallel"),
        name="conv_ffn",
    )(x2d, x2d, x2d, g_pre, w_up, conv_w, conv_b, w_down, g_post)


def _tile_sizes(bsz, seq):
    steps = 32 if seq % 32 == 0 else 1
    return steps * bsz, 256


def kernel(x, rel_bias, pre_mix_norm, w_in, lam_re, lam_im, log_step, b_re, b_im, c_re, c_im, ssm_d, w_glu,
           b_glu, attn_sink, ssm_out_norm, attn_out_norm, w_out, post_mix_norm, pre_ffn_norm, w_up, conv_w,
           conv_b, w_down, post_ffn_norm):
    bsz, seq, d_model = x.shape
    depth = w_in.shape[0]
    ssm_width = ssm_d.shape[1]
    attn_width = w_out.shape[1] - ssm_width
    g_cnt = ssm_width // SSM_GROUP
    nc = seq // SSM_CHUNK
    assert seq % BLOCK == 0 and bsz % 16 == 0 and ssm_width % LANES == 0
    assert attn_width == 2 * len(HEAD_ORDER[0]) * HEAD_DIM and rel_bias.shape == (N_BUCKETS, attn_width // HEAD_DIM)
    tm_ffn, ff_chunk = _tile_sizes(bsz, seq)
    assert w_down.shape[1] % ff_chunk == 0

    bias = _bias_table(rel_bias)
    ssm_ops = jax.vmap(_ssm_operators)(lam_re, lam_im, log_step, b_re, b_im, c_re, c_im, ssm_d)
    row = lambda a: a.astype(F32).reshape(1, -1)
    q_scale = jnp.concatenate([jnp.ones((ssm_width,), F32), jnp.full((attn_width,), HEAD_DIM ** -0.5, F32),
                               jnp.ones((w_in.shape[2] - ssm_width - attn_width,), F32)])

    x2d = x.reshape(bsz, seq * d_model)
    for l in range(depth):
        first, last = l == 0, l == depth - 1
        u_t, qkv = _proj(x2d, row(pre_mix_norm[l]), (w_in[l] * q_scale).astype(BF16), bsz, ssm_width, first)
        qkv3 = qkv.reshape(seq, bsz, -1).transpose(1, 0, 2)
        att = _attention(qkv3, bias, attn_sink[l].astype(F32), attn_width)
        att2d = att.transpose(1, 0, 2).reshape(seq * bsz, attn_width)

        z_t = _ssm(u_t.reshape(g_cnt, nc * bsz, -1), *(op[l] for op in ssm_ops), bsz).reshape(u_t.shape)

        x2d = _merge(z_t, att2d, x2d, w_glu[l].astype(BF16), row(b_glu[l]), row(ssm_out_norm[l]),
                     row(attn_out_norm[l]), w_out[l].astype(BF16), row(post_mix_norm[l]), bsz, first)
        x2d = _ffn(x2d, row(pre_ffn_norm[l]), w_up[l].astype(BF16), conv_w[l].astype(F32), row(conv_b[l]),
                   w_down[l].astype(BF16), row(post_ffn_norm[l]), bsz, tm_ffn, ff_chunk, last)
    return x2d.reshape(bsz, seq, d_model)
```

```python
import functools
import math

import numpy as np
import jax
import jax.numpy as jnp
from jax import lax
from jax.experimental import pallas as pl
from jax.experimental.pallas import tpu as pltpu

F32 = jnp.float32
BF16 = jnp.bfloat16

LANES = 128
SSM_GROUP = 16
SSM_STATE = 64
HEAD_DIM = 64
N_KV_HEADS = 2
WINDOW = 128
BLOCK = 128
N_BUCKETS = 32
MAX_DISTANCE = 128
EPS = 1e-6
NEG_INF = -1e30
SSM_CHUNK = 16
Q_BLOCKS = 4
GROUPS_PER_VREG = LANES // SSM_GROUP
CHUNKS_PER_STEP = 2

VMEM_LIMIT_BYTES = 56 * 1024 * 1024

HEAD_ORDER = ((0, 2, 5, 7), (1, 3, 4, 6))


def _params(*sem):
    return pltpu.CompilerParams(dimension_semantics=sem, vmem_limit_bytes=VMEM_LIMIT_BYTES)


def _rms(x, g):
    return x * lax.rsqrt(jnp.mean(x * x, axis=-1, keepdims=True) + EPS) * g


def _gelu(x):
    c = math.sqrt(2.0 / math.pi)
    return x * (0.5 + 0.5 * jnp.tanh(x * (c + (c * 0.044715) * (x * x))))


def _atom_transpose(vs):
    atom = lax.broadcasted_iota(jnp.int32, vs[0].shape, 1) // SSM_GROUP
    for d in (4, 2, 1):
        keep = (atom & d) == 0
        nxt = list(vs)
        for i in range(GROUPS_PER_VREG):
            if i & d == 0:
                a, b = vs[i], vs[i + d]
                nxt[i] = jnp.where(keep, a, pltpu.roll(b, SSM_GROUP * d, axis=1))
                nxt[i + d] = jnp.where(keep, pltpu.roll(a, LANES - SSM_GROUP * d, axis=1), b)
        vs = nxt
    return vs


def _bucket_table():
    half = N_BUCKETS // 2
    max_exact = half // 2
    qi = np.arange(BLOCK)[:, None]
    sj = np.arange(3 * BLOCK)[None, :]
    rel = sj - BLOCK - qi
    n = np.abs(rel)
    nf = np.maximum(n, 1).astype(np.float64)
    large = max_exact + (np.log(nf / max_exact) / math.log(MAX_DISTANCE / max_exact)
                         * (half - max_exact)).astype(np.int32)
    large = np.minimum(large, half - 1)
    bucket = np.where(rel > 0, half, 0) + np.where(n < max_exact, n, large)
    return np.where(n <= WINDOW, bucket, -1).astype(np.int32)


def _bias_kernel(bucket_ref, rel_bias_ref, out_ref):
    bucket = bucket_ref[...]
    col = lax.broadcasted_iota(jnp.int32, bucket.shape, 1)
    for grp, heads in enumerate(HEAD_ORDER):
        for slot, h in enumerate(heads):
            acc = jnp.full(bucket.shape, NEG_INF, F32)
            for b in range(N_BUCKETS):
                acc = jnp.where(bucket == b, rel_bias_ref[b, h], acc)
            for variant in range(4):
                lo = BLOCK if variant & 1 else 0
                hi = 2 * BLOCK if variant & 2 else 3 * BLOCK
                out_ref[variant, grp, slot * BLOCK:(slot + 1) * BLOCK, :] = jnp.where(
                    (col >= lo) & (col < hi), acc, NEG_INF)


def _bias_table(rel_bias):
    rows = len(HEAD_ORDER[0]) * BLOCK
    bucket = jnp.asarray(_bucket_table())
    return pl.pallas_call(
        _bias_kernel,
        out_shape=jax.ShapeDtypeStruct((4, len(HEAD_ORDER), rows, 3 * BLOCK), F32),
        in_specs=[pl.BlockSpec(memory_space=pltpu.VMEM), pl.BlockSpec(memory_space=pltpu.SMEM)],
        out_specs=pl.BlockSpec(memory_space=pltpu.VMEM),
        name="bias_table",
    )(bucket, rel_bias.astype(F32))


def _stream_spec(tm, d, bsz, batch_major):
    if batch_major:
        return pl.BlockSpec((bsz, (tm // bsz) * d), lambda i: (0, i))
    return pl.BlockSpec((tm, d), lambda i: (i, 0))


def _load_stream(x_ref, d, batch_major):
    if not batch_major:
        return x_ref[...]
    return jnp.concatenate([x_ref[:, t * d:(t + 1) * d] for t in range(x_ref.shape[1] // d)], axis=0)


def _proj_kernel(x_ref, g_ref, w_ref, ut_ref, qkv_ref, *, bsz, ssm_width, batch_major):
    h = _rms(_load_stream(x_ref, g_ref.shape[1], batch_major), g_ref[...]).astype(BF16)
    r = jnp.dot(h, w_ref[...], preferred_element_type=F32)
    qkv_ref[...] = r[:, ssm_width:].astype(BF16)
    for ck in range(ut_ref.shape[1]):
        for t1 in range(SSM_CHUNK // GROUPS_PER_VREG):
            step0 = ck * SSM_CHUNK + t1 * GROUPS_PER_VREG
            for g1 in range(ssm_width // LANES):
                vs = [r[(step0 + t2) * bsz:(step0 + t2 + 1) * bsz, g1 * LANES:(g1 + 1) * LANES]
                      for t2 in range(GROUPS_PER_VREG)]
                for g2, tile in enumerate(_atom_transpose(vs)):
                    ut_ref[g1 * GROUPS_PER_VREG + g2, ck, :, t1 * LANES:(t1 + 1) * LANES] = tile.astype(BF16)


def _proj(x2d, g, w, bsz, ssm_width, batch_major):
    d, n = w.shape
    t = x2d.size // d
    tm = CHUNKS_PER_STEP * SSM_CHUNK * bsz
    nc = t // (SSM_CHUNK * bsz)
    g_cnt = ssm_width // SSM_GROUP
    lanes = SSM_CHUNK * SSM_GROUP
    return pl.pallas_call(
        functools.partial(_proj_kernel, bsz=bsz, ssm_width=ssm_width, batch_major=batch_major),
        out_shape=(jax.ShapeDtypeStruct((g_cnt, nc, bsz, lanes), BF16),
                   jax.ShapeDtypeStruct((t, n - ssm_width), BF16)),
        grid=(t // tm,),
        in_specs=[_stream_spec(tm, d, bsz, batch_major),
                  pl.BlockSpec((1, d), lambda i: (0, 0)),
                  pl.BlockSpec((d, n), lambda i: (0, 0))],
        out_specs=(pl.BlockSpec((g_cnt, CHUNKS_PER_STEP, bsz, lanes), lambda i: (0, i, 0, 0)),
                   pl.BlockSpec((tm, n - ssm_width), lambda i: (i, 0))),
        compiler_params=_params("parallel"),
        name="proj",
    )(x2d, g, w)


def _attn_kernel(q_ref, *refs):
    n_kb = Q_BLOCKS + 2
    kv_refs = refs[:n_kb]
    bias_ref, sink_ref, o_ref = refs[n_kb:]
    j = pl.program_id(1)
    nj = pl.num_programs(1)
    kv_width = N_KV_HEADS * HEAD_DIM
    k = jnp.concatenate([r[:, :kv_width] for r in kv_refs], axis=0)
    v = jnp.concatenate([r[:, kv_width:] for r in kv_refs], axis=0)
    swap = lambda a: jnp.concatenate([a[:, HEAD_DIM:], a[:, :HEAD_DIM]], axis=1)
    k_sw, v_sw = swap(k), swap(v)
    low = lax.broadcasted_iota(jnp.int32, k.shape, 1) < HEAD_DIM
    zero = jnp.zeros_like(k)
    k_even = (jnp.where(low, k, zero), jnp.where(low, k_sw, zero))
    k_odd = (jnp.where(low, zero, k_sw), jnp.where(low, zero, k))
    ones = jnp.ones_like(v)
    v_ext = (jnp.concatenate([v, ones], axis=1), jnp.concatenate([v_sw, ones], axis=1))
    qk = lambda a, b: lax.dot_general(a, b, (((1,), (1,)), ((), ())), preferred_element_type=F32)
    low_o = lax.broadcasted_iota(jnp.int32, (BLOCK, LANES), 1) < HEAD_DIM
    for sub in range(Q_BLOCKS):
        variant = jnp.where(j == 0, 1, 0) if sub == 0 else 0
        variant = variant + (jnp.where(j == nj - 1, 2, 0) if sub == Q_BLOCKS - 1 else 0)
        qrows = slice(sub * BLOCK, (sub + 1) * BLOCK)
        keys = lambda a: a[sub * BLOCK:(sub + 3) * BLOCK]
        q_kv = (jnp.concatenate([q_ref[qrows, 0:LANES], q_ref[qrows, LANES:2 * LANES]], axis=0),
                jnp.concatenate([q_ref[qrows, 2 * LANES:3 * LANES], q_ref[qrows, 3 * LANES:4 * LANES]], axis=0))
        scores = (jnp.concatenate([qk(q_kv[0], keys(k_even[0])), qk(q_kv[1], keys(k_odd[1]))], axis=0),
                  jnp.concatenate([qk(q_kv[0], keys(k_odd[0])), qk(q_kv[1], keys(k_even[1]))], axis=0))
        outs = []
        for grp, heads in enumerate(HEAD_ORDER):
            sink = jnp.concatenate([jnp.full((BLOCK, LANES), sink_ref[h], F32) for h in heads], axis=0)
            s = scores[grp] + bias_ref[variant, grp]
            m = jnp.maximum(jnp.broadcast_to(jnp.max(s, axis=-1, keepdims=True), sink.shape), sink)
            e = jnp.concatenate([jnp.exp(s[:, c * LANES:(c + 1) * LANES] - m)
                                 for c in range(s.shape[1] // LANES)], axis=1).astype(BF16)
            o = jnp.dot(e, keys(v_ext[grp]), preferred_element_type=F32)
            outs.append(o[:, :LANES] / (o[:, LANES:] + jnp.exp(sink - m)))
        o_a, o_b = outs
        rows = lambda a, slot: a[slot * BLOCK:(slot + 1) * BLOCK]
        pairs = [jnp.where(low_o, rows(o_a, 0), rows(o_b, 0)), jnp.where(low_o, rows(o_a, 1), rows(o_b, 1)),
                 jnp.where(low_o, rows(o_b, 2), rows(o_a, 2)), jnp.where(low_o, rows(o_b, 3), rows(o_a, 3))]
        o_ref[qrows, :] = jnp.concatenate(pairs, axis=1).astype(BF16)


def _attention(qkv3, bias, sink, attn_width):
    bsz, seq, width = qkv3.shape
    kv_width = N_KV_HEADS * HEAD_DIM
    assert width == attn_width + 2 * kv_width and attn_width % (2 * kv_width) == 0
    assert seq % (Q_BLOCKS * BLOCK) == 0
    nb = seq // BLOCK
    rows = Q_BLOCKS * BLOCK
    kv_col = attn_width // (2 * kv_width)
    kv_spec = lambda off: pl.BlockSpec(
        (None, BLOCK, 2 * kv_width), lambda b, j: (b, jnp.clip(j * Q_BLOCKS + off, 0, nb - 1), kv_col))
    offsets = range(-1, Q_BLOCKS + 1)
    return pl.pallas_call(
        _attn_kernel,
        out_shape=jax.ShapeDtypeStruct((bsz, seq, attn_width), BF16),
        grid=(bsz, nb // Q_BLOCKS),
        in_specs=[pl.BlockSpec((None, rows, attn_width), lambda b, j: (b, j, 0))]
                 + [kv_spec(off) for off in offsets]
                 + [pl.BlockSpec(bias.shape, lambda b, j: (0, 0, 0, 0)), pl.BlockSpec(memory_space=pltpu.SMEM)],
        out_specs=pl.BlockSpec((None, rows, attn_width), lambda b, j: (b, j, 0)),
        compiler_params=_params("parallel", "parallel"),
        name="attention",
    )(*([qkv3] * (1 + len(offsets))), bias, sink)


def _ssm_operators(lam_re, lam_im, log_step, b_re, b_im, c_re, c_im, d):
    hp = lax.Precision.HIGH
    t_len = SSM_CHUNK
    lam_re, lam_im = lam_re.astype(F32), lam_im.astype(F32)
    dt = jnp.exp(log_step.astype(F32))[..., None]
    ld_re, ld_im = lam_re * dt, lam_im * dt
    mag = jnp.exp(ld_re)
    lb_re, lb_im = mag * jnp.cos(ld_im), mag * jnp.sin(ld_im)
    den = lam_re * lam_re + lam_im * lam_im
    nr, ni = lb_re - 1.0, lb_im
    coef_re = (nr * lam_re + ni * lam_im) / den
    coef_im = (ni * lam_re - nr * lam_im) / den
    b_re, b_im = b_re.astype(F32), b_im.astype(F32)
    bb_re = coef_re[..., None] * b_re - coef_im[..., None] * b_im
    bb_im = coef_re[..., None] * b_im + coef_im[..., None] * b_re
    g_cnt, n_cnt, p_cnt = bb_re.shape[1:]

    def powers(re, im, tau):
        m = jnp.exp(re * tau)
        return m * jnp.cos(im * tau), m * jnp.sin(im * tau)

    tau = jnp.arange(t_len + 1, dtype=F32)
    pw_re, pw_im = powers(ld_re[..., None], ld_im[..., None], tau)
    pwt_re, pwt_im = powers(ld_re[:, :, None], ld_im[:, :, None], tau[:, None])

    lanes = t_len * p_cnt
    ct_re = jnp.tile(c_re.astype(F32).transpose(0, 1, 3, 2), (1, 1, 1, t_len))
    ct_im = jnp.tile(c_im.astype(F32).transpose(0, 1, 3, 2), (1, 1, 1, t_len))

    def c_times(direction, pr, pi):
        pr, pi = jnp.repeat(pr, p_cnt, axis=-1), jnp.repeat(pi, p_cnt, axis=-1)
        cr, ci = ct_re[direction], ct_im[direction]
        return cr * pr - ci * pi, cr * pi + ci * pr

    k_lag = []
    for direction in range(2):
        cp_re, cp_im = c_times(direction, pw_re[direction][..., :t_len], pw_im[direction][..., :t_len])
        k_lag.append(jnp.einsum('gnq,gnm->gqm',
                                jnp.concatenate([bb_re[direction], -bb_im[direction]], axis=1),
                                jnp.concatenate([cp_re, cp_im], axis=1), precision=hp))
    skip = d.astype(F32).reshape(g_cnt, 1, p_cnt) * jnp.eye(p_cnt, dtype=F32)
    k_fwd = k_lag[0].at[:, :, :p_cnt].add(skip)
    k_bwd_rev = k_lag[1].reshape(g_cnt, p_cnt, t_len, p_cnt)[:, :, ::-1].reshape(g_cnt, p_cnt, lanes)
    zeros = jnp.zeros_like(k_fwd)
    f_pad = jnp.concatenate([zeros, k_fwd], axis=2)
    b_pad = jnp.concatenate([k_bwd_rev, zeros], axis=2)
    kbig = jnp.stack([f_pad[:, :, (t_len - s) * p_cnt:(2 * t_len - s) * p_cnt]
                      + b_pad[:, :, (t_len - 1 - s) * p_cnt:(2 * t_len - 1 - s) * p_cnt]
                      for s in range(t_len)], axis=1)
    kbig = kbig.reshape(g_cnt, lanes, lanes)

    def lift_b(direction, pr, pi):
        br = jnp.tile(bb_re[direction].transpose(0, 2, 1), (1, t_len, 1))
        bi = jnp.tile(bb_im[direction].transpose(0, 2, 1), (1, t_len, 1))
        pr, pi = jnp.repeat(pr, p_cnt, axis=1), jnp.repeat(pi, p_cnt, axis=1)
        return pr * br - pi * bi, pr * bi + pi * br

    bf_re, bf_im = lift_b(0, pwt_re[0][:, :t_len][:, ::-1], pwt_im[0][:, :t_len][:, ::-1])
    bw_re, bw_im = lift_b(1, pwt_re[1][:, :t_len], pwt_im[1][:, :t_len])
    bbig = jnp.concatenate([bf_re, bw_re, bf_im, bw_im], axis=-1)

    cf_re, cf_im = c_times(0, pw_re[0][..., 1:], pw_im[0][..., 1:])
    cb_re, cb_im = c_times(1, pw_re[1][..., :0:-1], pw_im[1][..., :0:-1])
    cbig = jnp.concatenate([cf_re, cb_re, -cf_im, -cb_im], axis=1)

    a_re = jnp.concatenate([pw_re[0][..., t_len], pw_re[1][..., t_len]], axis=-1)
    a_im = jnp.concatenate([pw_im[0][..., t_len], pw_im[1][..., t_len]], axis=-1)
    a16 = jnp.stack([a_re, a_im], axis=1)
    return bbig.astype(BF16), kbig.astype(BF16), cbig.astype(BF16), a16


def _ssm_kernel(u_ref, bbig_ref, kbig_ref, cbig_ref, a_ref, z_ref, v_scr, s_scr, *, bsz, rows_per_dot):
    rows = u_ref.shape[1]
    nc = rows // bsz
    half = 2 * SSM_STATE
    blocks = [slice(r, r + rows_per_dot) for r in range(0, rows, rows_per_dot)]

    for blk in blocks:
        v_scr[blk, :] = jnp.dot(u_ref[0, blk, :], bbig_ref[0], preferred_element_type=F32)

    a_re = a_ref[0, 0:1, :]
    a_im = a_ref[0, 1:2, :]
    is_fwd = lax.broadcasted_iota(jnp.int32, (bsz, half), 1) < SSM_STATE

    def scan_step(i, carry):
        s_re, s_im = carry
        rf = pl.multiple_of(i * bsz, bsz)
        rb = pl.multiple_of((nc - 1 - i) * bsz, bsz)
        s_scr[pl.ds(rf, bsz), 0:half] = s_re
        s_scr[pl.ds(rf, bsz), half:2 * half] = s_im
        v_re = jnp.where(is_fwd, v_scr[pl.ds(rf, bsz), 0:half], v_scr[pl.ds(rb, bsz), 0:half])
        v_im = jnp.where(is_fwd, v_scr[pl.ds(rf, bsz), half:2 * half], v_scr[pl.ds(rb, bsz), half:2 * half])
        n_re = a_re * s_re - a_im * s_im + v_re
        n_im = a_re * s_im + a_im * s_re + v_im
        return n_re, n_im

    zero = jnp.zeros((bsz, half), F32)
    lax.fori_loop(0, nc, scan_step, (zero, zero), unroll=2)

    is_fwd2 = lax.broadcasted_iota(jnp.int32, (bsz, 2 * half), 1) % half < SSM_STATE
    chunk = lambda c: s_scr[c * bsz:(c + 1) * bsz, :]
    for blk in blocks:
        entering = jnp.concatenate([jnp.where(is_fwd2, chunk(c), chunk(nc - 1 - c))
                                    for c in range(blk.start // bsz, blk.stop // bsz)], axis=0)
        y = (jnp.dot(u_ref[0, blk, :], kbig_ref[0], preferred_element_type=F32)
             + jnp.dot(entering.astype(BF16), cbig_ref[0], preferred_element_type=F32))
        z_ref[0, blk, :] = _gelu(y).astype(BF16)


def _ssm(u_t, bbig, kbig, cbig, a16, bsz):
    g_cnt, rows, width = u_t.shape
    rows_per_dot = min(rows, 512)
    n_state = bbig.shape[-1]
    return pl.pallas_call(
        functools.partial(_ssm_kernel, bsz=bsz, rows_per_dot=rows_per_dot),
        out_shape=jax.ShapeDtypeStruct((g_cnt, rows, width), BF16),
        grid=(g_cnt,),
        in_specs=[pl.BlockSpec((1, rows, width), lambda g: (g, 0, 0)),
                  pl.BlockSpec((1, width, n_state), lambda g: (g, 0, 0)),
                  pl.BlockSpec((1, width, width), lambda g: (g, 0, 0)),
                  pl.BlockSpec((1, n_state, width), lambda g: (g, 0, 0)),
                  pl.BlockSpec((1, 2, n_state // 2), lambda g: (g, 0, 0))],
        out_specs=pl.BlockSpec((1, rows, width), lambda g: (g, 0, 0)),
        scratch_shapes=[pltpu.VMEM((rows, n_state), F32),
                        pltpu.VMEM((rows, n_state), F32)],
        compiler_params=_params("parallel"),
        name="s5_mixer",
    )(u_t, bbig, kbig, cbig, a16)


def _merge_kernel(zt_ref, att_ref, x_ref, wglu_ref, bglu_ref, gssm_ref, gatt_ref, wout_ref, gpost_ref, o_ref,
                  z_scr, *, bsz, batch_major):
    for ck in range(zt_ref.shape[1]):
        for t1 in range(SSM_CHUNK // GROUPS_PER_VREG):
            for g1 in range(z_scr.shape[1] // LANES):
                vs = [zt_ref[g1 * GROUPS_PER_VREG + g2, ck, :, t1 * LANES:(t1 + 1) * LANES].astype(F32)
                      for g2 in range(GROUPS_PER_VREG)]
                for t2, tile in enumerate(_atom_transpose(vs)):
                    t = ck * SSM_CHUNK + t1 * GROUPS_PER_VREG + t2
                    z_scr[t * bsz:(t + 1) * bsz, g1 * LANES:(g1 + 1) * LANES] = tile
    z = z_scr[...]
    gate = jnp.dot(z.astype(BF16), wglu_ref[...], preferred_element_type=F32) + bglu_ref[...]
    y_ssm = z * jax.nn.sigmoid(gate)
    y_att = att_ref[...].astype(F32)
    merged = jnp.concatenate([_rms(y_ssm, gssm_ref[...]).astype(BF16),
                              _rms(y_att, gatt_ref[...]).astype(BF16)], axis=-1)
    y = jnp.dot(merged, wout_ref[...], preferred_element_type=F32)
    o_ref[...] = _load_stream(x_ref, o_ref.shape[1], batch_major) + _rms(y, gpost_ref[...])


def _merge(z_t, att2d, x2d, w_glu, b_glu, g_ssm, g_att, w_out, g_post, bsz, batch_major):
    d = w_out.shape[1]
    t = x2d.size // d
    g_cnt, nc, _, lanes = z_t.shape
    ws, wa = g_cnt * SSM_GROUP, att2d.shape[1]
    tm = CHUNKS_PER_STEP * SSM_CHUNK * bsz
    row = lambda w: pl.BlockSpec((tm, w), lambda i: (i, 0))
    full = lambda a: pl.BlockSpec(a.shape, lambda i: (0,) * a.ndim)
    return pl.pallas_call(
        functools.partial(_merge_kernel, bsz=bsz, batch_major=batch_major),
        out_shape=jax.ShapeDtypeStruct((t, d), F32),
        grid=(t // tm,),
        in_specs=[pl.BlockSpec((g_cnt, CHUNKS_PER_STEP, bsz, lanes), lambda i: (0, i, 0, 0)),
                  row(wa), _stream_spec(tm, d, bsz, batch_major), full(w_glu), full(b_glu), full(g_ssm), full(g_att),
                  full(w_out), full(g_post)],
        out_specs=row(d),
        scratch_shapes=[pltpu.VMEM((tm, ws), F32)],
        compiler_params=_params("parallel"),
        name="merge",
    )(z_t, att2d, x2d, w_glu, b_glu, g_ssm, g_att, w_out, g_post)


def _ffn_kernel(xp_ref, x_ref, xn_ref, gpre_ref, wup_ref, cw_ref, cb_ref, wdown_ref, gpost_ref, o_ref,
                h_scr, a_scr, act_scr, *, d_ff, ff_chunk, halo, out_batch_major):
    i = pl.program_id(0)
    n_tiles = pl.num_programs(0)
    tm = x_ref.shape[0]
    keep_prev = jnp.where(i == 0, 0.0, 1.0)
    keep_next = jnp.where(i == n_tiles - 1, 0.0, 1.0)
    g = gpre_ref[...]
    h_scr[0:halo, :] = (_rms(xp_ref[...], g) * keep_prev).astype(BF16)
    h_scr[halo:halo + tm, :] = _rms(x_ref[...], g).astype(BF16)
    h_scr[halo + tm:, :] = (_rms(xn_ref[...], g) * keep_next).astype(BF16)
    n_chunks = d_ff // ff_chunk

    def cols(base, c):
        return pl.ds(base + c * ff_chunk, ff_chunk)

    def up(c, slot):
        h = h_scr[...]
        for part, base in enumerate((0, d_ff)):
            a_scr[slot, part] = jnp.dot(h, wup_ref[:, cols(base, c)], preferred_element_type=F32)

    def post(c, slot):
        def conv(part, base):
            w = cw_ref[:, cols(base, c)]
            return (a_scr[slot, part, 0:tm] * w[0:1] + a_scr[slot, part, halo:halo + tm] * w[1:2]
                    + a_scr[slot, part, 2 * halo:2 * halo + tm] * w[2:3] + cb_ref[:, cols(base, c)])

        act = _gelu(conv(1, d_ff)) * conv(0, 0)
        act_scr[:, cols(0, c)] = act.astype(BF16)

    n_slots = a_scr.shape[0]
    up(0, 0)
    for c in range(n_chunks - 1):
        up(c + 1, (c + 1) % n_slots)
        post(c, c % n_slots)
    k_head = (n_chunks - 1) * ff_chunk
    y = jnp.dot(act_scr[:, :k_head], wdown_ref[:k_head, :], preferred_element_type=F32)
    post(n_chunks - 1, (n_chunks - 1) % n_slots)
    y = y + jnp.dot(act_scr[:, k_head:], wdown_ref[k_head:, :], preferred_element_type=F32)
    out = x_ref[...] + _rms(y, gpost_ref[...])
    if out_batch_major:
        d = out.shape[1]
        for t in range(tm // halo):
            o_ref[:, t * d:(t + 1) * d] = out[t * halo:(t + 1) * halo]
    else:
        o_ref[...] = out


def _ffn(x2d, g_pre, w_up, conv_w, conv_b, w_down, g_post, bsz, tm, ff_chunk, out_batch_major):
    t, d = x2d.shape
    d_ff = w_down.shape[0]
    halo = bsz
    hb = tm // halo
    n_halo_blocks = t // halo
    full = lambda a: pl.BlockSpec(a.shape, lambda i: (0,) * a.ndim)
    return pl.pallas_call(
        functools.partial(_ffn_kernel, d_ff=d_ff, ff_chunk=ff_chunk, halo=halo, out_batch_major=out_batch_major),
        out_shape=jax.ShapeDtypeStruct((bsz, (t // bsz) * d) if out_batch_major else (t, d), F32),
        grid=(t // tm,),
        in_specs=[pl.BlockSpec((halo, d), lambda i: (jnp.maximum(i * hb - 1, 0), 0)),
                  pl.BlockSpec((tm, d), lambda i: (i, 0)),
                  pl.BlockSpec((halo, d), lambda i: (jnp.minimum((i + 1) * hb, n_halo_blocks - 1), 0)),
                  full(g_pre), full(w_up), full(conv_w), full(conv_b), full(w_down), full(g_post)],
        out_specs=_stream_spec(tm, d, bsz, out_batch_major),
        scratch_shapes=[pltpu.VMEM((tm + 2 * halo, d), BF16),
                        pltpu.VMEM((3, 2, tm + 2 * halo, ff_chunk), F32),
                        pltpu.VMEM((tm, d_ff), BF16)],
        compiler_params=_params("parallel"),
        name="conv_ffn",
    )(x2d, x2d, x2d, g_pre, w_up, conv_w, conv_b, w_down, g_post)


def _tile_sizes(bsz, seq):
    steps = 32 if seq % 32 == 0 else 1
    return steps * bsz, 256


def kernel(x, rel_bias, pre_mix_norm, w_in, lam_re, lam_im, log_step, b_re, b_im, c_re, c_im, ssm_d, w_glu,
           b_glu, attn_sink, ssm_out_norm, attn_out_norm, w_out, post_mix_norm, pre_ffn_norm, w_up, conv_w,
           conv_b, w_down, post_ffn_norm):
    bsz, seq, d_model = x.shape
    depth = w_in.shape[0]
    ssm_width = ssm_d.shape[1]
    attn_width = w_out.shape[1] - ssm_width
    g_cnt = ssm_width // SSM_GROUP
    nc = seq // SSM_CHUNK
    assert seq % BLOCK == 0 and bsz % 16 == 0 and ssm_width % LANES == 0
    assert attn_width == 2 * len(HEAD_ORDER[0]) * HEAD_DIM and rel_bias.shape == (N_BUCKETS, attn_width // HEAD_DIM)
    tm_ffn, ff_chunk = _tile_sizes(bsz, seq)
    assert w_down.shape[1] % ff_chunk == 0

    bias = _bias_table(rel_bias)
    ssm_ops = jax.vmap(_ssm_operators)(lam_re, lam_im, log_step, b_re, b_im, c_re, c_im, ssm_d)
    row = lambda a: a.astype(F32).reshape(1, -1)
    q_scale = jnp.concatenate([jnp.ones((ssm_width,), F32), jnp.full((attn_width,), HEAD_DIM ** -0.5, F32),
                               jnp.ones((w_in.shape[2] - ssm_width - attn_width,), F32)])

    x2d = x.reshape(bsz, seq * d_model)
    for l in range(depth):
        first, last = l == 0, l == depth - 1
        u_t, qkv = _proj(x2d, row(pre_mix_norm[l]), (w_in[l] * q_scale).astype(BF16), bsz, ssm_width, first)
        qkv3 = qkv.reshape(seq, bsz, -1).transpose(1, 0, 2)
        att = _attention(qkv3, bias, attn_sink[l].astype(F32), attn_width)
        att2d = att.transpose(1, 0, 2).reshape(seq * bsz, attn_width)

        z_t = _ssm(u_t.reshape(g_cnt, nc * bsz, -1), *(op[l] for op in ssm_ops), bsz).reshape(u_t.shape)

        x2d = _merge(z_t, att2d, x2d, w_glu[l].astype(BF16), row(b_glu[l]), row(ssm_out_norm[l]),
                     row(attn_out_norm[l]), w_out[l].astype(BF16), row(post_mix_norm[l]), bsz, first)
        x2d = _ffn(x2d, row(pre_ffn_norm[l]), w_up[l].astype(BF16), conv_w[l].astype(F32), row(conv_b[l]),
                   w_down[l].astype(BF16), row(post_ffn_norm[l]), bsz, tm_ffn, ff_chunk, last)
    return x2d.reshape(bsz, seq, d_model)
```

```python
import functools
import math

import numpy as np
import jax
import jax.numpy as jnp
from jax import lax
from jax.experimental import pallas as pl
from jax.experimental.pallas import tpu as pltpu

F32 = jnp.float32
BF16 = jnp.bfloat16

LANES = 128
SSM_GROUP = 16
SSM_STATE = 64
HEAD_DIM = 64
N_KV_HEADS = 2
WINDOW = 128
BLOCK = 128
N_BUCKETS = 32
MAX_DISTANCE = 128
EPS = 1e-6
NEG_INF = -1e30
SSM_CHUNK = 16
Q_BLOCKS = 8
GROUPS_PER_VREG = LANES // SSM_GROUP
CHUNKS_PER_STEP = 2

VMEM_LIMIT_BYTES = 56 * 1024 * 1024

HEAD_ORDER = ((0, 2, 5, 7), (1, 3, 4, 6))


def _params(*sem):
    return pltpu.CompilerParams(dimension_semantics=sem, vmem_limit_bytes=VMEM_LIMIT_BYTES)


def _rms(x, g):
    return x * lax.rsqrt(jnp.mean(x * x, axis=-1, keepdims=True) + EPS) * g


def _gelu(x):
    c = math.sqrt(2.0 / math.pi)
    return x * (0.5 + 0.5 * jnp.tanh(x * (c + (c * 0.044715) * (x * x))))


def _atom_transpose(vs):
    atom = lax.broadcasted_iota(jnp.int32, vs[0].shape, 1) // SSM_GROUP
    for d in (4, 2, 1):
        keep = (atom & d) == 0
        nxt = list(vs)
        for i in range(GROUPS_PER_VREG):
            if i & d == 0:
                a, b = vs[i], vs[i + d]
                nxt[i] = jnp.where(keep, a, pltpu.roll(b, SSM_GROUP * d, axis=1))
                nxt[i + d] = jnp.where(keep, pltpu.roll(a, LANES - SSM_GROUP * d, axis=1), b)
        vs = nxt
    return vs


def _bucket_table():
    half = N_BUCKETS // 2
    max_exact = half // 2
    qi = np.arange(BLOCK)[:, None]
    sj = np.arange(3 * BLOCK)[None, :]
    rel = sj - BLOCK - qi
    n = np.abs(rel)
    nf = np.maximum(n, 1).astype(np.float64)
    large = max_exact + (np.log(nf / max_exact) / math.log(MAX_DISTANCE / max_exact)
                         * (half - max_exact)).astype(np.int32)
    large = np.minimum(large, half - 1)
    bucket = np.where(rel > 0, half, 0) + np.where(n < max_exact, n, large)
    return np.where(n <= WINDOW, bucket, -1).astype(np.int32)


def _bias_kernel(bucket_ref, rel_bias_ref, out_ref):
    bucket = bucket_ref[...]
    col = lax.broadcasted_iota(jnp.int32, bucket.shape, 1)
    for grp, heads in enumerate(HEAD_ORDER):
        for slot, h in enumerate(heads):
            acc = jnp.full(bucket.shape, NEG_INF, F32)
            for b in range(N_BUCKETS):
                acc = jnp.where(bucket == b, rel_bias_ref[b, h], acc)
            for variant in range(4):
                lo = BLOCK if variant & 1 else 0
                hi = 2 * BLOCK if variant & 2 else 3 * BLOCK
                out_ref[variant, grp, slot * BLOCK:(slot + 1) * BLOCK, :] = jnp.where(
                    (col >= lo) & (col < hi), acc, NEG_INF)


def _bias_table(rel_bias):
    rows = len(HEAD_ORDER[0]) * BLOCK
    bucket = jnp.asarray(_bucket_table())
    return pl.pallas_call(
        _bias_kernel,
        out_shape=jax.ShapeDtypeStruct((4, len(HEAD_ORDER), rows, 3 * BLOCK), F32),
        in_specs=[pl.BlockSpec(memory_space=pltpu.VMEM), pl.BlockSpec(memory_space=pltpu.SMEM)],
        out_specs=pl.BlockSpec(memory_space=pltpu.VMEM),
        name="bias_table",
    )(bucket, rel_bias.astype(F32))


def _stream_spec(tm, d, bsz, batch_major):
    if batch_major:
        return pl.BlockSpec((bsz, (tm // bsz) * d), lambda i: (0, i))
    return pl.BlockSpec((tm, d), lambda i: (i, 0))


def _load_stream(x_ref, d, batch_major):
    if not batch_major:
        return x_ref[...]
    return jnp.concatenate([x_ref[:, t * d:(t + 1) * d] for t in range(x_ref.shape[1] // d)], axis=0)


def _proj_kernel(x_ref, g_ref, w_ref, ut_ref, qkv_ref, *, bsz, ssm_width, batch_major):
    h = _rms(_load_stream(x_ref, g_ref.shape[1], batch_major), g_ref[...]).astype(BF16)
    r = jnp.dot(h, w_ref[...], preferred_element_type=F32)
    qkv_ref[...] = r[:, ssm_width:].astype(BF16)
    for ck in range(ut_ref.shape[1]):
        for t1 in range(SSM_CHUNK // GROUPS_PER_VREG):
            step0 = ck * SSM_CHUNK + t1 * GROUPS_PER_VREG
            for g1 in range(ssm_width // LANES):
                vs = [r[(step0 + t2) * bsz:(step0 + t2 + 1) * bsz, g1 * LANES:(g1 + 1) * LANES]
                      for t2 in range(GROUPS_PER_VREG)]
                for g2, tile in enumerate(_atom_transpose(vs)):
                    ut_ref[g1 * GROUPS_PER_VREG + g2, ck, :, t1 * LANES:(t1 + 1) * LANES] = tile.astype(BF16)


def _proj(x2d, g, w, bsz, ssm_width, batch_major):
    d, n = w.shape
    t = x2d.size // d
    tm = CHUNKS_PER_STEP * SSM_CHUNK * bsz
    nc = t // (SSM_CHUNK * bsz)
    g_cnt = ssm_width // SSM_GROUP
    lanes = SSM_CHUNK * SSM_GROUP
    return pl.pallas_call(
        functools.partial(_proj_kernel, bsz=bsz, ssm_width=ssm_width, batch_major=batch_major),
        out_shape=(jax.ShapeDtypeStruct((g_cnt, nc, bsz, lanes), BF16),
                   jax.ShapeDtypeStruct((t, n - ssm_width), BF16)),
        grid=(t // tm,),
        in_specs=[_stream_spec(tm, d, bsz, batch_major),
                  pl.BlockSpec((1, d), lambda i: (0, 0)),
                  pl.BlockSpec((d, n), lambda i: (0, 0))],
        out_specs=(pl.BlockSpec((g_cnt, CHUNKS_PER_STEP, bsz, lanes), lambda i: (0, i, 0, 0)),
                   pl.BlockSpec((tm, n - ssm_width), lambda i: (i, 0))),
        compiler_params=_params("parallel"),
        name="proj",
    )(x2d, g, w)


def _attn_kernel(q_ref, *refs):
    n_kb = Q_BLOCKS + 2
    kv_refs = refs[:n_kb]
    bias_ref, sink_ref, o_ref = refs[n_kb:]
    j = pl.program_id(1)
    nj = pl.num_programs(1)
    kv_width = N_KV_HEADS * HEAD_DIM
    k = jnp.concatenate([r[:, :kv_width] for r in kv_refs], axis=0)
    v = jnp.concatenate([r[:, kv_width:] for r in kv_refs], axis=0)
    swap = lambda a: jnp.concatenate([a[:, HEAD_DIM:], a[:, :HEAD_DIM]], axis=1)
    k_sw, v_sw = swap(k), swap(v)
    low = lax.broadcasted_iota(jnp.int32, k.shape, 1) < HEAD_DIM
    zero = jnp.zeros_like(k)
    k_even = (jnp.where(low, k, zero), jnp.where(low, k_sw, zero))
    k_odd = (jnp.where(low, zero, k_sw), jnp.where(low, zero, k))
    ones = jnp.ones_like(v)
    v_ext = (jnp.concatenate([v, ones], axis=1), jnp.concatenate([v_sw, ones], axis=1))
    qk = lambda a, b: lax.dot_general(a, b, (((1,), (1,)), ((), ())), preferred_element_type=F32)
    low_o = lax.broadcasted_iota(jnp.int32, (BLOCK, LANES), 1) < HEAD_DIM
    for sub in range(Q_BLOCKS):
        variant = jnp.where(j == 0, 1, 0) if sub == 0 else 0
        variant = variant + (jnp.where(j == nj - 1, 2, 0) if sub == Q_BLOCKS - 1 else 0)
        qrows = slice(sub * BLOCK, (sub + 1) * BLOCK)
        keys = lambda a: a[sub * BLOCK:(sub + 3) * BLOCK]
        q_kv = (jnp.concatenate([q_ref[qrows, 0:LANES], q_ref[qrows, LANES:2 * LANES]], axis=0),
                jnp.concatenate([q_ref[qrows, 2 * LANES:3 * LANES], q_ref[qrows, 3 * LANES:4 * LANES]], axis=0))
        scores = (jnp.concatenate([qk(q_kv[0], keys(k_even[0])), qk(q_kv[1], keys(k_odd[1]))], axis=0),
                  jnp.concatenate([qk(q_kv[0], keys(k_odd[0])), qk(q_kv[1], keys(k_even[1]))], axis=0))
        outs = []
        for grp, heads in enumerate(HEAD_ORDER):
            sink = jnp.concatenate([jnp.full((BLOCK, LANES), sink_ref[h], F32) for h in heads], axis=0)
            s = scores[grp] + bias_ref[variant, grp]
            m = jnp.maximum(jnp.broadcast_to(jnp.max(s, axis=-1, keepdims=True), sink.shape), sink)
            e = jnp.concatenate([jnp.exp(s[:, c * LANES:(c + 1) * LANES] - m)
                                 for c in range(s.shape[1] // LANES)], axis=1).astype(BF16)
            o = jnp.dot(e, keys(v_ext[grp]), preferred_element_type=F32)
            outs.append(o[:, :LANES] / (o[:, LANES:] + jnp.exp(sink - m)))
        o_a, o_b = outs
        rows = lambda a, slot: a[slot * BLOCK:(slot + 1) * BLOCK]
        pairs = [jnp.where(low_o, rows(o_a, 0), rows(o_b, 0)), jnp.where(low_o, rows(o_a, 1), rows(o_b, 1)),
                 jnp.where(low_o, rows(o_b, 2), rows(o_a, 2)), jnp.where(low_o, rows(o_b, 3), rows(o_a, 3))]
        o_ref[qrows, :] = jnp.concatenate(pairs, axis=1).astype(BF16)


def _attention(qkv3, bias, sink, attn_width):
    bsz, seq, width = qkv3.shape
    kv_width = N_KV_HEADS * HEAD_DIM
    assert width == attn_width + 2 * kv_width and attn_width % (2 * kv_width) == 0
    assert seq % (Q_BLOCKS * BLOCK) == 0
    nb = seq // BLOCK
    rows = Q_BLOCKS * BLOCK
    kv_col = attn_width // (2 * kv_width)
    kv_spec = lambda off: pl.BlockSpec(
        (None, BLOCK, 2 * kv_width), lambda b, j: (b, jnp.clip(j * Q_BLOCKS + off, 0, nb - 1), kv_col))
    offsets = range(-1, Q_BLOCKS + 1)
    return pl.pallas_call(
        _attn_kernel,
        out_shape=jax.ShapeDtypeStruct((bsz, seq, attn_width), BF16),
        grid=(bsz, nb // Q_BLOCKS),
        in_specs=[pl.BlockSpec((None, rows, attn_width), lambda b, j: (b, j, 0))]
                 + [kv_spec(off) for off in offsets]
                 + [pl.BlockSpec(bias.shape, lambda b, j: (0, 0, 0, 0)), pl.BlockSpec(memory_space=pltpu.SMEM)],
        out_specs=pl.BlockSpec((None, rows, attn_width), lambda b, j: (b, j, 0)),
        compiler_params=_params("parallel", "parallel"),
        name="attention",
    )(*([qkv3] * (1 + len(offsets))), bias, sink)


def _ssm_operators(lam_re, lam_im, log_step, b_re, b_im, c_re, c_im, d):
    hp = lax.Precision.HIGH
    t_len = SSM_CHUNK
    lam_re, lam_im = lam_re.astype(F32), lam_im.astype(F32)
    dt = jnp.exp(log_step.astype(F32))[..., None]
    ld_re, ld_im = lam_re * dt, lam_im * dt
    mag = jnp.exp(ld_re)
    lb_re, lb_im = mag * jnp.cos(ld_im), mag * jnp.sin(ld_im)
    den = lam_re * lam_re + lam_im * lam_im
    nr, ni = lb_re - 1.0, lb_im
    coef_re = (nr * lam_re + ni * lam_im) / den
    coef_im = (ni * lam_re - nr * lam_im) / den
    b_re, b_im = b_re.astype(F32), b_im.astype(F32)
    bb_re = coef_re[..., None] * b_re - coef_im[..., None] * b_im
    bb_im = coef_re[..., None] * b_im + coef_im[..., None] * b_re
    g_cnt, n_cnt, p_cnt = bb_re.shape[1:]

    def powers(re, im, tau):
        m = jnp.exp(re * tau)
        return m * jnp.cos(im * tau), m * jnp.sin(im * tau)

    tau = jnp.arange(t_len + 1, dtype=F32)
    pw_re, pw_im = powers(ld_re[..., None], ld_im[..., None], tau)
    pwt_re, pwt_im = powers(ld_re[:, :, None], ld_im[:, :, None], tau[:, None])

    lanes = t_len * p_cnt
    ct_re = jnp.tile(c_re.astype(F32).transpose(0, 1, 3, 2), (1, 1, 1, t_len))
    ct_im = jnp.tile(c_im.astype(F32).transpose(0, 1, 3, 2), (1, 1, 1, t_len))

    def c_times(direction, pr, pi):
        pr, pi = jnp.repeat(pr, p_cnt, axis=-1), jnp.repeat(pi, p_cnt, axis=-1)
        cr, ci = ct_re[direction], ct_im[direction]
        return cr * pr - ci * pi, cr * pi + ci * pr

    k_lag = []
    for direction in range(2):
        cp_re, cp_im = c_times(direction, pw_re[direction][..., :t_len], pw_im[direction][..., :t_len])
        k_lag.append(jnp.einsum('gnq,gnm->gqm',
                                jnp.concatenate([bb_re[direction], -bb_im[direction]], axis=1),
                                jnp.concatenate([cp_re, cp_im], axis=1), precision=hp))
    skip = d.astype(F32).reshape(g_cnt, 1, p_cnt) * jnp.eye(p_cnt, dtype=F32)
    k_fwd = k_lag[0].at[:, :, :p_cnt].add(skip)
    k_bwd_rev = k_lag[1].reshape(g_cnt, p_cnt, t_len, p_cnt)[:, :, ::-1].reshape(g_cnt, p_cnt, lanes)
    zeros = jnp.zeros_like(k_fwd)
    f_pad = jnp.concatenate([zeros, k_fwd], axis=2)
    b_pad = jnp.concatenate([k_bwd_rev, zeros], axis=2)
    kbig = jnp.stack([f_pad[:, :, (t_len - s) * p_cnt:(2 * t_len - s) * p_cnt]
                      + b_pad[:, :, (t_len - 1 - s) * p_cnt:(2 * t_len - 1 - s) * p_cnt]
                      for s in range(t_len)], axis=1)
    kbig = kbig.reshape(g_cnt, lanes, lanes)

    def lift_b(direction, pr, pi):
        br = jnp.tile(bb_re[direction].transpose(0, 2, 1), (1, t_len, 1))
        bi = jnp.tile(bb_im[direction].transpose(0, 2, 1), (1, t_len, 1))
        pr, pi = jnp.repeat(pr, p_cnt, axis=1), jnp.repeat(pi, p_cnt, axis=1)
        return pr * br - pi * bi, pr * bi + pi * br

    bf_re, bf_im = lift_b(0, pwt_re[0][:, :t_len][:, ::-1], pwt_im[0][:, :t_len][:, ::-1])
    bw_re, bw_im = lift_b(1, pwt_re[1][:, :t_len], pwt_im[1][:, :t_len])
    bbig = jnp.concatenate([bf_re, bw_re, bf_im, bw_im], axis=-1)

    cf_re, cf_im = c_times(0, pw_re[0][..., 1:], pw_im[0][..., 1:])
    cb_re, cb_im = c_times(1, pw_re[1][..., :0:-1], pw_im[1][..., :0:-1])
    cbig = jnp.concatenate([cf_re, cb_re, -cf_im, -cb_im], axis=1)

    a_re = jnp.concatenate([pw_re[0][..., t_len], pw_re[1][..., t_len]], axis=-1)
    a_im = jnp.concatenate([pw_im[0][..., t_len], pw_im[1][..., t_len]], axis=-1)
    a16 = jnp.stack([a_re, a_im], axis=1)
    return bbig.astype(BF16), kbig.astype(BF16), cbig.astype(BF16), a16


def _ssm_kernel(u_ref, bbig_ref, kbig_ref, cbig_ref, a_ref, z_ref, v_scr, s_scr, *, bsz, rows_per_dot):
    rows = u_ref.shape[1]
    nc = rows // bsz
    half = 2 * SSM_STATE
    blocks = [slice(r, r + rows_per_dot) for r in range(0, rows, rows_per_dot)]

    for blk in blocks:
        v_scr[blk, :] = jnp.dot(u_ref[0, blk, :], bbig_ref[0], preferred_element_type=F32)

    a_re = a_ref[0, 0:1, :]
    a_im = a_ref[0, 1:2, :]
    is_fwd = lax.broadcasted_iota(jnp.int32, (bsz, half), 1) < SSM_STATE

    def scan_step(i, carry):
        s_re, s_im = carry
        rf = pl.multiple_of(i * bsz, bsz)
        rb = pl.multiple_of((nc - 1 - i) * bsz, bsz)
        s_scr[pl.ds(rf, bsz), 0:half] = s_re
        s_scr[pl.ds(rf, bsz), half:2 * half] = s_im
        v_re = jnp.where(is_fwd, v_scr[pl.ds(rf, bsz), 0:half], v_scr[pl.ds(rb, bsz), 0:half])
        v_im = jnp.where(is_fwd, v_scr[pl.ds(rf, bsz), half:2 * half], v_scr[pl.ds(rb, bsz), half:2 * half])
        n_re = a_re * s_re - a_im * s_im + v_re
        n_im = a_re * s_im + a_im * s_re + v_im
        return n_re, n_im

    zero = jnp.zeros((bsz, half), F32)
    lax.fori_loop(0, nc, scan_step, (zero, zero), unroll=2)

    is_fwd2 = lax.broadcasted_iota(jnp.int32, (bsz, 2 * half), 1) % half < SSM_STATE
    chunk = lambda c: s_scr[c * bsz:(c + 1) * bsz, :]
    for blk in blocks:
        entering = jnp.concatenate([jnp.where(is_fwd2, chunk(c), chunk(nc - 1 - c))
                                    for c in range(blk.start // bsz, blk.stop // bsz)], axis=0)
        y = (jnp.dot(u_ref[0, blk, :], kbig_ref[0], preferred_element_type=F32)
             + jnp.dot(entering.astype(BF16), cbig_ref[0], preferred_element_type=F32))
        z_ref[0, blk, :] = _gelu(y).astype(BF16)


def _ssm(u_t, bbig, kbig, cbig, a16, bsz):
    g_cnt, rows, width = u_t.shape
    rows_per_dot = min(rows, 512)
    n_state = bbig.shape[-1]
    return pl.pallas_call(
        functools.partial(_ssm_kernel, bsz=bsz, rows_per_dot=rows_per_dot),
        out_shape=jax.ShapeDtypeStruct((g_cnt, rows, width), BF16),
        grid=(g_cnt,),
        in_specs=[pl.BlockSpec((1, rows, width), lambda g: (g, 0, 0)),
                  pl.BlockSpec((1, width, n_state), lambda g: (g, 0, 0)),
                  pl.BlockSpec((1, width, width), lambda g: (g, 0, 0)),
                  pl.BlockSpec((1, n_state, width), lambda g: (g, 0, 0)),
                  pl.BlockSpec((1, 2, n_state // 2), lambda g: (g, 0, 0))],
        out_specs=pl.BlockSpec((1, rows, width), lambda g: (g, 0, 0)),
        scratch_shapes=[pltpu.VMEM((rows, n_state), F32),
                        pltpu.VMEM((rows, n_state), F32)],
        compiler_params=_params("parallel"),
        name="s5_mixer",
    )(u_t, bbig, kbig, cbig, a16)


def _merge_kernel(zt_ref, att_ref, x_ref, wglu_ref, bglu_ref, gssm_ref, gatt_ref, wout_ref, gpost_ref, o_ref,
                  z_scr, *, bsz, batch_major):
    for ck in range(zt_ref.shape[1]):
        for t1 in range(SSM_CHUNK // GROUPS_PER_VREG):
            for g1 in range(z_scr.shape[1] // LANES):
                vs = [zt_ref[g1 * GROUPS_PER_VREG + g2, ck, :, t1 * LANES:(t1 + 1) * LANES].astype(F32)
                      for g2 in range(GROUPS_PER_VREG)]
                for t2, tile in enumerate(_atom_transpose(vs)):
                    t = ck * SSM_CHUNK + t1 * GROUPS_PER_VREG + t2
                    z_scr[t * bsz:(t + 1) * bsz, g1 * LANES:(g1 + 1) * LANES] = tile
    z = z_scr[...]
    gate = jnp.dot(z.astype(BF16), wglu_ref[...], preferred_element_type=F32) + bglu_ref[...]
    y_ssm = z * jax.nn.sigmoid(gate)
    y_att = att_ref[...].astype(F32)
    merged = jnp.concatenate([_rms(y_ssm, gssm_ref[...]).astype(BF16),
                              _rms(y_att, gatt_ref[...]).astype(BF16)], axis=-1)
    y = jnp.dot(merged, wout_ref[...], preferred_element_type=F32)
    o_ref[...] = _load_stream(x_ref, o_ref.shape[1], batch_major) + _rms(y, gpost_ref[...])


def _merge(z_t, att2d, x2d, w_glu, b_glu, g_ssm, g_att, w_out, g_post, bsz, batch_major):
    d = w_out.shape[1]
    t = x2d.size // d
    g_cnt, nc, _, lanes = z_t.shape
    ws, wa = g_cnt * SSM_GROUP, att2d.shape[1]
    tm = CHUNKS_PER_STEP * SSM_CHUNK * bsz
    row = lambda w: pl.BlockSpec((tm, w), lambda i: (i, 0))
    full = lambda a: pl.BlockSpec(a.shape, lambda i: (0,) * a.ndim)
    return pl.pallas_call(
        functools.partial(_merge_kernel, bsz=bsz, batch_major=batch_major),
        out_shape=jax.ShapeDtypeStruct((t, d), F32),
        grid=(t // tm,),
        in_specs=[pl.BlockSpec((g_cnt, CHUNKS_PER_STEP, bsz, lanes), lambda i: (0, i, 0, 0)),
                  row(wa), _stream_spec(tm, d, bsz, batch_major), full(w_glu), full(b_glu), full(g_ssm), full(g_att),
                  full(w_out), full(g_post)],
        out_specs=row(d),
        scratch_shapes=[pltpu.VMEM((tm, ws), F32)],
        compiler_params=_params("parallel"),
        name="merge",
    )(z_t, att2d, x2d, w_glu, b_glu, g_ssm, g_att, w_out, g_post)


def _ffn_kernel(xp_ref, x_ref, xn_ref, gpre_ref, wup_ref, cw_ref, cb_ref, wdown_ref, gpost_ref, o_ref,
                h_scr, a_scr, act_scr, *, d_ff, ff_chunk, halo, out_batch_major):
    i = pl.program_id(0)
    n_tiles = pl.num_programs(0)
    tm = x_ref.shape[0]
    keep_prev = jnp.where(i == 0, 0.0, 1.0)
    keep_next = jnp.where(i == n_tiles - 1, 0.0, 1.0)
    g = gpre_ref[...]
    h_scr[0:halo, :] = (_rms(xp_ref[...], g) * keep_prev).astype(BF16)
    h_scr[halo:halo + tm, :] = _rms(x_ref[...], g).astype(BF16)
    h_scr[halo + tm:, :] = (_rms(xn_ref[...], g) * keep_next).astype(BF16)
    n_chunks = d_ff // ff_chunk

    def cols(base, c):
        return pl.ds(base + c * ff_chunk, ff_chunk)

    def up(c, slot):
        h = h_scr[...]
        for part, base in enumerate((0, d_ff)):
            a_scr[slot, part] = jnp.dot(h, wup_ref[:, cols(base, c)], preferred_element_type=F32)

    def post(c, slot):
        def conv(part, base):
            w = cw_ref[:, cols(base, c)]
            return (a_scr[slot, part, 0:tm] * w[0:1] + a_scr[slot, part, halo:halo + tm] * w[1:2]
                    + a_scr[slot, part, 2 * halo:2 * halo + tm] * w[2:3] + cb_ref[:, cols(base, c)])

        act = _gelu(conv(1, d_ff)) * conv(0, 0)
        act_scr[:, cols(0, c)] = act.astype(BF16)

    n_slots = a_scr.shape[0]
    up(0, 0)
    for c in range(n_chunks - 1):
        up(c + 1, (c + 1) % n_slots)
        post(c, c % n_slots)
    k_head = (n_chunks - 1) * ff_chunk
    y = jnp.dot(act_scr[:, :k_head], wdown_ref[:k_head, :], preferred_element_type=F32)
    post(n_chunks - 1, (n_chunks - 1) % n_slots)
    y = y + jnp.dot(act_scr[:, k_head:], wdown_ref[k_head:, :], preferred_element_type=F32)
    out = x_ref[...] + _rms(y, gpost_ref[...])
    if out_batch_major:
        d = out.shape[1]
        for t in range(tm // halo):
            o_ref[:, t * d:(t + 1) * d] = out[t * halo:(t + 1) * halo]
    else:
        o_ref[...] = out


def _ffn(x2d, g_pre, w_up, conv_w, conv_b, w_down, g_post, bsz, tm, ff_chunk, out_batch_major):
    t, d = x2d.shape
    d_ff = w_down.shape[0]
    halo = bsz
    hb = tm // halo
    n_halo_blocks = t // halo
    full = lambda a: pl.BlockSpec(a.shape, lambda i: (0,) * a.ndim)
    return pl.pallas_call(
        functools.partial(_ffn_kernel, d_ff=d_ff, ff_chunk=ff_chunk, halo=halo, out_batch_major=out_batch_major),
        out_shape=jax.ShapeDtypeStruct((bsz, (t // bsz) * d) if out_batch_major else (t, d), F32),
        grid=(t // tm,),
        in_specs=[pl.BlockSpec((halo, d), lambda i: (jnp.maximum(i * hb - 1, 0), 0)),
                  pl.BlockSpec((tm, d), lambda i: (i, 0)),
                  pl.BlockSpec((halo, d), lambda i: (jnp.minimum((i + 1) * hb, n_halo_blocks - 1), 0)),
                  full(g_pre), full(w_up), full(conv_w), full(conv_b), full(w_down), full(g_post)],
        out_specs=_stream_spec(tm, d, bsz, out_batch_major),
        scratch_shapes=[pltpu.VMEM((tm + 2 * halo, d), BF16),
                        pltpu.VMEM((3, 2, tm + 2 * halo, ff_chunk), F32),
                        pltpu.VMEM((tm, d_ff), BF16)],
        compiler_params=_params("parallel"),
        name="conv_ffn",
    )(x2d, x2d, x2d, g_pre, w_up, conv_w, conv_b, w_down, g_post)


def _tile_sizes(bsz, seq):
    steps = 32 if seq % 32 == 0 else 1
    return steps * bsz, 256


def kernel(x, rel_bias, pre_mix_norm, w_in, lam_re, lam_im, log_step, b_re, b_im, c_re, c_im, ssm_d, w_glu,
           b_glu, attn_sink, ssm_out_norm, attn_out_norm, w_out, post_mix_norm, pre_ffn_norm, w_up, conv_w,
           conv_b, w_down, post_ffn_norm):
    bsz, seq, d_model = x.shape
    depth = w_in.shape[0]
    ssm_width = ssm_d.shape[1]
    attn_width = w_out.shape[1] - ssm_width
    g_cnt = ssm_width // SSM_GROUP
    nc = seq // SSM_CHUNK
    assert seq % BLOCK == 0 and bsz % 16 == 0 and ssm_width % LANES == 0
    assert attn_width == 2 * len(HEAD_ORDER[0]) * HEAD_DIM and rel_bias.shape == (N_BUCKETS, attn_width // HEAD_DIM)
    tm_ffn, ff_chunk = _tile_sizes(bsz, seq)
    assert w_down.shape[1] % ff_chunk == 0

    bias = _bias_table(rel_bias)
    ssm_ops = jax.vmap(_ssm_operators)(lam_re, lam_im, log_step, b_re, b_im, c_re, c_im, ssm_d)
    row = lambda a: a.astype(F32).reshape(1, -1)
    q_scale = jnp.concatenate([jnp.ones((ssm_width,), F32), jnp.full((attn_width,), HEAD_DIM ** -0.5, F32),
                               jnp.ones((w_in.shape[2] - ssm_width - attn_width,), F32)])

    x2d = x.reshape(bsz, seq * d_model)
    for l in range(depth):
        first, last = l == 0, l == depth - 1
        u_t, qkv = _proj(x2d, row(pre_mix_norm[l]), (w_in[l] * q_scale).astype(BF16), bsz, ssm_width, first)
        qkv3 = qkv.reshape(seq, bsz, -1).transpose(1, 0, 2)
        att = _attention(qkv3, bias, attn_sink[l].astype(F32), attn_width)
        att2d = att.transpose(1, 0, 2).reshape(seq * bsz, attn_width)

        z_t = _ssm(u_t.reshape(g_cnt, nc * bsz, -1), *(op[l] for op in ssm_ops), bsz).reshape(u_t.shape)

        x2d = _merge(z_t, att2d, x2d, w_glu[l].astype(BF16), row(b_glu[l]), row(ssm_out_norm[l]),
                     row(attn_out_norm[l]), w_out[l].astype(BF16), row(post_mix_norm[l]), bsz, first)
        x2d = _ffn(x2d, row(pre_ffn_norm[l]), w_up[l].astype(BF16), conv_w[l].astype(F32), row(conv_b[l]),
                   w_down[l].astype(BF16), row(post_ffn_norm[l]), bsz, tm_ffn, ff_chunk, last)
    return x2d.reshape(bsz, seq, d_model)
```
